```python
import jax
import jax.numpy as jnp
from jax import lax
import numpy as np

D_MODEL = 2048
BATCH = 4
SEQ = 2048
DEPTH = 2
DEC_BATCH = 128
DEC_SEQ = 8
PAST_LEN = 2048
PAGE_SIZE = 128

HEAD_DIM = 64
D_MIX = D_MODEL
D_SB = D_MIX // 2
D_RW = D_MIX - D_SB
H_SB = D_SB // HEAD_DIM
H_RW = D_RW // HEAD_DIM
R_DECAY = 64
R_AAA = 64
R_GATE = 128
N_SHIFT = 3 * D_RW + R_DECAY + R_AAA + R_GATE
RW_SPLITS = (D_RW, 2 * D_RW, 3 * D_RW, 3 * D_RW + R_DECAY, 3 * D_RW + R_DECAY + R_AAA)
N_IN = 3 * D_SB + N_SHIFT
D_FF = 5632
N_EXPERTS = 8
TOP_K = 2
D_EXPERT = D_FF // TOP_K
N_DENSE = (DEPTH + 1) // 2
N_MOE = DEPTH // 2
Q_BLOCK = 128
RMS_EPS = 1e-6
LN_X_EPS = 64e-5
POOL_NUM = 5
POOL_DEN = 4
SB_BIAS_INIT = -6.0

kernel_name = 'hybrid_rwkv7_stickbreaking_decode_step'


def rms_norm(x, g):
    xf = x.astype(jnp.float32)
    y = xf * lax.rsqrt(jnp.mean(xf * xf, axis=-1, keepdims=True) + RMS_EPS)
    return (y * g.astype(jnp.float32)).astype(x.dtype)


def stick_breaking_block(q, k, v, q_pos, k_pos, bias):
    z = jnp.einsum('bqhd,bkhd->bhqk', q, k).astype(jnp.float32) * (HEAD_DIM ** -0.5)
    z = z + bias.astype(jnp.float32)[None, :, None, None]
    mask = k_pos[None, :] < q_pos[:, None]
    log_beta = jax.nn.log_sigmoid(z)
    log_1m = jnp.where(mask, jax.nn.log_sigmoid(-z), 0.0)
    suffix = lax.cumsum(log_1m, axis=3, reverse=True) - log_1m
    a = jnp.where(mask, jnp.exp(log_beta + suffix), 0.0)
    return jnp.einsum('bhqk,bkhd->bqhd', a.astype(v.dtype), v)


def stick_breaking_attend(q, k, v, past_k, past_v, bias):
    t = q.shape[1]
    p = past_k.shape[1]
    k_all = jnp.concatenate([past_k, k], axis=1)
    v_all = jnp.concatenate([past_v, v], axis=1)
    outs = []
    for lo in range(0, t, Q_BLOCK):
        hi = min(lo + Q_BLOCK, t)
        outs.append(stick_breaking_block(q[:, lo:hi], k_all[:, :p + hi], v_all[:, :p + hi],
                                         p + jnp.arange(lo, hi), jnp.arange(p + hi), bias))
    return jnp.concatenate(outs, axis=1)


def rwkv7_step(s, inp):
    r_t, w_t, k_t, v_t, a_t, b_t = inp
    sa = jnp.einsum('bhij,bhj->bhi', s, a_t)
    s = s * w_t[:, :, None, :] + sa[..., None] * b_t[:, :, None, :] + v_t[..., None] * k_t[:, :, None, :]
    return s, jnp.einsum('bhij,bhj->bhi', s, r_t)


def rwkv7_mix(p_rw, shift_prev, wkv_prev, W, l):
    b, t, _ = p_rw.shape
    f32 = jnp.float32
    prev = jnp.concatenate([shift_prev[:, None, :].astype(p_rw.dtype), p_rw[:, :-1]], axis=1)
    xs = p_rw + (prev - p_rw) * W['rw_shift_mu'][l]
    r, k, v, wd, ad, gd = jnp.split(xs, RW_SPLITS, axis=-1)
    w_log = -jax.nn.softplus(-(W['rw_w0'][l] + jnp.tanh(wd) @ W['rw_w2'][l])) - 0.5
    decay = jnp.exp(-jnp.exp(w_log.astype(f32)))
    a = jax.nn.sigmoid(W['rw_a0'][l] + ad @ W['rw_a2'][l])
    g = jax.nn.sigmoid(gd) @ W['rw_g2'][l]
    heads = lambda z: z.reshape(b, t, H_RW, HEAD_DIM).astype(f32)
    kk = heads(k * W['rw_k_k'][l])
    kk = kk * lax.rsqrt(jnp.maximum(jnp.sum(kk * kk, axis=-1, keepdims=True), 1e-24))
    k = k * (1.0 + (a - 1.0) * W['rw_k_a'][l])
    rh, kh, vh, ah, wh = heads(r), heads(k), heads(v), heads(a), heads(decay)
    seq = tuple(jnp.moveaxis(z, 1, 0) for z in (rh, wh, kh, vh, -kk, kk * ah))
    wkv, y = lax.scan(rwkv7_step, wkv_prev.astype(f32), seq)
    y = jnp.moveaxis(y, 0, 1)
    mu = jnp.mean(y, axis=-1, keepdims=True)
    var = jnp.mean(jnp.square(y - mu), axis=-1, keepdims=True)
    y = ((y - mu) * lax.rsqrt(var + LN_X_EPS)).reshape(b, t, D_RW)
    y = y * W['rw_ln_g'][l].astype(f32) + W['rw_ln_b'][l].astype(f32)
    bonus = jnp.sum(rh * kh * W['rw_r_k'][l].astype(f32), axis=-1, keepdims=True) * vh
    y = y + bonus.reshape(b, t, D_RW)
    return (y * g.astype(f32)).astype(p_rw.dtype), wkv.astype(wkv_prev.dtype), p_rw[:, -1]


def swiglu(h, wg, wu, wd):
    return (jax.nn.silu(h @ wg) * (h @ wu)) @ wd


def moe_swiglu(h, router, wg, wu, wd):
    probs = jax.nn.softmax((h @ router).astype(jnp.float32), axis=-1)
    top_p, top_i = lax.top_k(probs, TOP_K)
    top_p = top_p / jnp.sum(top_p, axis=-1, keepdims=True)
    gates = jnp.einsum('btk,btke->bte', top_p, jax.nn.one_hot(top_i, N_EXPERTS, dtype=jnp.float32))
    out = jnp.zeros_like(h)
    for e in range(N_EXPERTS):
        out = out + swiglu(h, wg[e], wu[e], wd[e]) * gates[..., e:e + 1].astype(h.dtype)
    return out


def hybrid_layer(x, l, past_k, past_v, shift_prev, wkv_prev, W):
    b, t, _ = x.shape
    h = rms_norm(x, W['norm_mix_g'][l])
    proj = h @ W['w_in'][l]
    q = proj[..., :D_SB].reshape(b, t, H_SB, HEAD_DIM)
    k = proj[..., D_SB:2 * D_SB].reshape(b, t, H_SB, HEAD_DIM)
    v = proj[..., 2 * D_SB:3 * D_SB].reshape(b, t, H_SB, HEAD_DIM)
    p_rw = proj[..., 3 * D_SB:]
    q = rms_norm(q, W['sb_q_norm_g'][l])
    k = rms_norm(k, W['sb_k_norm_g'][l])
    o_sb = stick_breaking_attend(q, k, v, past_k, past_v, W['sb_logit_bias'][l]).reshape(b, t, D_SB)
    o_rw, wkv_new, shift_new = rwkv7_mix(p_rw, shift_prev, wkv_prev, W, l)
    o = jnp.concatenate([o_sb * W['sb_out_scale'][l], o_rw], axis=-1)
    x = x + o @ W['w_out'][l]
    h = rms_norm(x, W['norm_ffn_g'][l])
    i = l // 2
    if l % 2 == 0:
        f = swiglu(h, W['ffn_w_gate'][i], W['ffn_w_up'][i], W['ffn_w_down'][i])
    else:
        f = moe_swiglu(h, W['moe_router'][i], W['moe_w_gate'][i], W['moe_w_up'][i], W['moe_w_down'][i])
    return x + f, k, v, wkv_new, shift_new


def setup_inputs(seed: int = 0) -> dict:
    key = jax.random.key(seed)
    ks = jax.random.split(key, 33)
    f32 = jnp.float32
    n_pages = PAST_LEN // PAGE_SIZE
    n_pool = (DEC_BATCH * n_pages * POOL_NUM) // POOL_DEN

    def nrm(k, shape, scale):
        return jax.random.normal(k, shape, f32) * scale

    page_table = jax.random.permutation(ks[4], n_pool)[:DEC_BATCH * n_pages]
    page_table = page_table.reshape(DEC_BATCH, n_pages).astype(jnp.int32)
    return {
        'x_prompt': nrm(ks[0], (BATCH, SEQ, D_MODEL), 1.0),
        'x_sample': nrm(ks[1], (DEC_BATCH, DEC_SEQ, D_MODEL), 1.0),
        'cache_k': nrm(ks[2], (DEPTH, n_pool, PAGE_SIZE, H_SB, HEAD_DIM), 1.0),
        'cache_v': nrm(ks[3], (DEPTH, n_pool, PAGE_SIZE, H_SB, HEAD_DIM), 1.0),
        'page_table': page_table,
        'state_wkv': nrm(ks[5], (DEPTH, DEC_BATCH, H_RW, HEAD_DIM, HEAD_DIM), 0.1),
        'state_shift': nrm(ks[6], (DEPTH, DEC_BATCH, N_SHIFT), 1.0),
        'norm_mix_g': 1.0 + nrm(ks[7], (DEPTH, D_MODEL), 0.02),
        'w_in': nrm(ks[8], (DEPTH, D_MODEL, N_IN), D_MODEL ** -0.5),
        'sb_q_norm_g': 1.0 + nrm(ks[9], (DEPTH, HEAD_DIM), 0.02),
        'sb_k_norm_g': 1.0 + nrm(ks[10], (DEPTH, HEAD_DIM), 0.02),
        'sb_logit_bias': SB_BIAS_INIT + nrm(ks[32], (DEPTH, H_SB), 0.1),
        'sb_out_scale': 1.0 + nrm(ks[11], (DEPTH, D_SB), 0.02),
        'rw_shift_mu': jax.random.uniform(ks[12], (DEPTH, N_SHIFT), f32),
        'rw_w0': jax.random.uniform(ks[13], (DEPTH, D_RW), f32, -2.0, 1.0),
        'rw_w2': nrm(ks[14], (DEPTH, R_DECAY, D_RW), 0.5 * R_DECAY ** -0.5),
        'rw_a0': nrm(ks[15], (DEPTH, D_RW), 0.5),
        'rw_a2': nrm(ks[16], (DEPTH, R_AAA, D_RW), R_AAA ** -0.5),
        'rw_g2': nrm(ks[17], (DEPTH, R_GATE, D_RW), R_GATE ** -0.5),
        'rw_k_k': 1.0 + nrm(ks[18], (DEPTH, D_RW), 0.1),
        'rw_k_a': 1.0 + nrm(ks[19], (DEPTH, D_RW), 0.1),
        'rw_r_k': nrm(ks[20], (DEPTH, H_RW, HEAD_DIM), 0.1),
        'rw_ln_g': 1.0 + nrm(ks[21], (DEPTH, D_RW), 0.02),
        'rw_ln_b': nrm(ks[22], (DEPTH, D_RW), 0.02),
        'w_out': nrm(ks[23], (DEPTH, D_MIX, D_MODEL), D_MIX ** -0.5),
        'norm_ffn_g': 1.0 + nrm(ks[24], (DEPTH, D_MODEL), 0.02),
        'ffn_w_gate': nrm(ks[25], (N_DENSE, D_MODEL, D_FF), D_MODEL ** -0.5),
        'ffn_w_up': nrm(ks[26], (N_DENSE, D_MODEL, D_FF), D_MODEL ** -0.5),
        'ffn_w_down': nrm(ks[27], (N_DENSE, D_FF, D_MODEL), D_FF ** -0.5),
        'moe_router': nrm(ks[28], (N_MOE, D_MODEL, N_EXPERTS), D_MODEL ** -0.5),
        'moe_w_gate': nrm(ks[29], (N_MOE, N_EXPERTS, D_MODEL, D_EXPERT), D_MODEL ** -0.5),
        'moe_w_up': nrm(ks[30], (N_MOE, N_EXPERTS, D_MODEL, D_EXPERT), D_MODEL ** -0.5),
        'moe_w_down': nrm(ks[31], (N_MOE, N_EXPERTS, D_EXPERT, D_MODEL), D_EXPERT ** -0.5),
    }


def reference(x_prompt, x_sample, cache_k, cache_v, page_table, state_wkv, state_shift,
              norm_mix_g, w_in, sb_q_norm_g, sb_k_norm_g, sb_logit_bias, sb_out_scale, rw_shift_mu,
              rw_w0, rw_w2, rw_a0, rw_a2, rw_g2, rw_k_k, rw_k_a, rw_r_k, rw_ln_g, rw_ln_b,
              w_out, norm_ffn_g, ffn_w_gate, ffn_w_up, ffn_w_down,
              moe_router, moe_w_gate, moe_w_up, moe_w_down):
    W = dict(norm_mix_g=norm_mix_g, w_in=w_in, sb_q_norm_g=sb_q_norm_g, sb_k_norm_g=sb_k_norm_g,
             sb_logit_bias=sb_logit_bias, sb_out_scale=sb_out_scale, rw_shift_mu=rw_shift_mu,
             rw_w0=rw_w0, rw_w2=rw_w2, rw_a0=rw_a0, rw_a2=rw_a2, rw_g2=rw_g2, rw_k_k=rw_k_k,
             rw_k_a=rw_k_a, rw_r_k=rw_r_k, rw_ln_g=rw_ln_g, rw_ln_b=rw_ln_b, w_out=w_out,
             norm_ffn_g=norm_ffn_g, ffn_w_gate=ffn_w_gate, ffn_w_up=ffn_w_up, ffn_w_down=ffn_w_down,
             moe_router=moe_router, moe_w_gate=moe_w_gate, moe_w_up=moe_w_up, moe_w_down=moe_w_down)
    bp = x_prompt.shape[0]
    nb, n_pages = page_table.shape
    past_len = n_pages * PAGE_SIZE
    empty_kv = jnp.zeros((bp, 0, H_SB, HEAD_DIM), x_prompt.dtype)
    shift0 = jnp.zeros((bp, N_SHIFT), x_prompt.dtype)
    wkv0 = jnp.zeros((bp, H_RW, HEAD_DIM, HEAD_DIM), jnp.float32)
    xp, xs = x_prompt, x_sample
    kp, vp, ksm, vsm, wp, wsm, sp, ssm = [], [], [], [], [], [], [], []
    for l in range(DEPTH):
        xp, k_l, v_l, wkv_l, sh_l = hybrid_layer(xp, l, empty_kv, empty_kv, shift0, wkv0, W)
        kp.append(k_l); vp.append(v_l); wp.append(wkv_l); sp.append(sh_l)
        past_k = cache_k[l][page_table].reshape(nb, past_len, H_SB, HEAD_DIM)
        past_v = cache_v[l][page_table].reshape(nb, past_len, H_SB, HEAD_DIM)
        xs, k_l, v_l, wkv_l, sh_l = hybrid_layer(xs, l, past_k, past_v, state_shift[l], state_wkv[l], W)
        ksm.append(k_l); vsm.append(v_l); wsm.append(wkv_l); ssm.append(sh_l)
    return (xp, xs, jnp.stack(kp), jnp.stack(vp), jnp.stack(ksm), jnp.stack(vsm),
            jnp.stack(wp), jnp.stack(wsm), jnp.stack(sp), jnp.stack(ssm))
```

```python
import functools

import jax
import jax.numpy as jnp
from jax import lax
from jax.experimental import pallas as pl
from jax.experimental.pallas import tpu as pltpu

f32 = jnp.float32
bf16 = jnp.bfloat16

D_MODEL = 2048
HEAD_DIM = 64
N_HEADS = 16
D_GRP = N_HEADS * HEAD_DIM
R_DECAY = 64
R_AAA = 64
R_GATE = 128
N_SHIFT = 3 * D_GRP + R_DECAY + R_AAA + R_GATE
N_IN = 3 * D_GRP + N_SHIFT
N_EXPERTS = 8
PAGE = 128
RMS_EPS = 1e-6
LN_X_EPS = 64e-5
QK_SCALE = HEAD_DIM ** -0.5

LANES = 128
VMEM_LIMIT = 56 * 1024 * 1024

NN = (((1,), (0,)), ((), ()))
NT = (((1,), (1,)), ((), ()))
TN = (((0,), (0,)), ((), ()))


def _params(*sem):
    return pltpu.CompilerParams(dimension_semantics=sem, vmem_limit_bytes=VMEM_LIMIT)


def _split(x, n):
    out, r = [], x
    for i in range(n):
        h = r.astype(bf16)
        out.append(h)
        if i + 1 < n:
            r = r - h.astype(f32)
    return out


def _mm(a, b, dn=NN, pa=1, pb=1):
    sa = [a] if a.dtype == bf16 else _split(a, pa)
    sb = [b] if b.dtype == bf16 else _split(b, pb)
    order = max(len(sa), len(sb))
    acc = None
    for i, ai in enumerate(sa):
        for j, bj in enumerate(sb):
            if i + j < order:
                t = lax.dot_general(ai, bj, dn, preferred_element_type=f32)
                acc = t if acc is None else acc + t
    return acc


def _softplus(z):
    return jnp.maximum(z, 0.0) + jnp.log1p(jnp.exp(-jnp.abs(z)))


def _iota(shape, dim):
    return lax.broadcasted_iota(jnp.int32, shape, dim)


def _norm_proj_body(x_ref, g_ref, w_ref, o_ref, h_scr):
    @pl.when(pl.program_id(1) == 0)
    def _():
        x = x_ref[...]
        y = x * lax.rsqrt(jnp.mean(x * x, axis=-1, keepdims=True) + RMS_EPS)
        h_scr[...] = (y * g_ref[...]).astype(bf16)

    o_ref[...] = jnp.dot(h_scr[...], w_ref[...], preferred_element_type=f32)


def _norm_proj(x, g, w_bf16, tm=512, tn=640):
    t, d = x.shape
    n = w_bf16.shape[1]
    return pl.pallas_call(
        _norm_proj_body,
        grid=(t // tm, n // tn),
        in_specs=[pl.BlockSpec((tm, d), lambda i, j: (i, 0)),
                  pl.BlockSpec((1, d), lambda i, j: (0, 0)),
                  pl.BlockSpec((d, tn), lambda i, j: (0, j))],
        out_specs=pl.BlockSpec((tm, tn), lambda i, j: (i, j)),
        out_shape=jax.ShapeDtypeStruct((t, n), f32),
        scratch_shapes=[pltpu.VMEM((tm, d), bf16)],
        compiler_params=_params("parallel", "arbitrary"),
        name="norm_proj",
    )(x, g.reshape(1, d), w_bf16)


def _mix_prep_body(proj_ref, prev_ref, mu_ref, gq_ref, gk_ref, w0_ref, w2_ref, a0_ref, a2_ref, g2_ref,
                   kk_ref, ka_ref, r1_ref, r2_ref,
                   q_ref, k_ref, r_ref, lw_ref, k2_ref, v_ref, a_ref, b_ref, g_ref):
    seg_ones = r1_ref[...]
    seg_bcast = r2_ref[...]

    def head_sum(x):
        return _mm(x, seg_ones, NN, 3, 1)

    def head_spread(s):
        return _mm(s, seg_bcast, NN, 3, 1)

    q = proj_ref[:, 0:D_GRP]
    qn = q * head_spread(lax.rsqrt(head_sum(q * q) * (1.0 / HEAD_DIM) + RMS_EPS))
    q_ref[...] = qn * gq_ref[...] * QK_SCALE
    k = proj_ref[:, D_GRP:2 * D_GRP]
    kn = k * head_spread(lax.rsqrt(head_sum(k * k) * (1.0 / HEAD_DIM) + RMS_EPS))
    k_ref[...] = kn * gk_ref[...]

    base = 3 * D_GRP

    def shifted(lo, hi):
        p = proj_ref[:, base + lo:base + hi]
        return p + (prev_ref[:, lo:hi] - p) * mu_ref[:, lo:hi]

    r = shifted(0, D_GRP)
    kr = shifted(D_GRP, 2 * D_GRP)
    v = shifted(2 * D_GRP, 3 * D_GRP)
    tail = shifted(3 * D_GRP, N_SHIFT)
    wd = tail[:, 0:R_DECAY]
    ad = tail[:, R_DECAY:R_DECAY + R_AAA]
    gd = tail[:, R_DECAY + R_AAA:]

    w_pre = w0_ref[...] + _mm(jnp.tanh(wd), w2_ref[...])
    w_log = -_softplus(-w_pre) - 0.5
    lw_ref[...] = -jnp.exp(w_log)
    a_lr = jax.nn.sigmoid(a0_ref[...] + _mm(ad, a2_ref[...]))
    g_ref[...] = _mm(jax.nn.sigmoid(gd), g2_ref[...])

    kk = kr * kk_ref[...]
    kk = kk * head_spread(lax.rsqrt(jnp.maximum(head_sum(kk * kk), 1e-24)))
    r_ref[...] = r
    k2_ref[...] = kr * (1.0 + (a_lr - 1.0) * ka_ref[...])
    v_ref[...] = v
    a_ref[...] = -kk
    b_ref[...] = kk * a_lr


def _mix_prep(proj, prev, mu, gq, gk, w0, w2, a0, a2, g2, k_k, k_a, tm=128):
    t = proj.shape[0]
    heads = _iota((D_GRP, LANES), 0) // HEAD_DIM
    seg_ones = (heads == _iota((D_GRP, LANES), 1)).astype(bf16)
    seg_bcast = seg_ones.T
    row = lambda n: pl.BlockSpec((1, n), lambda i: (0, 0))
    full = lambda a: pl.BlockSpec(a.shape, lambda i: (0, 0))
    tok = lambda n: pl.BlockSpec((tm, n), lambda i: (i, 0))
    outs = pl.pallas_call(
        _mix_prep_body,
        grid=(t // tm,),
        in_specs=[tok(N_IN), tok(N_SHIFT), row(N_SHIFT), row(D_GRP), row(D_GRP), row(D_GRP), full(w2), row(D_GRP),
                  full(a2), full(g2), row(D_GRP), row(D_GRP), full(seg_ones), full(seg_bcast)],
        out_specs=[tok(D_GRP)] * 9,
        out_shape=[jax.ShapeDtypeStruct((t, D_GRP), f32)] * 9,
        compiler_params=_params("parallel"),
        name="mix_prep",
    )(proj, prev, mu.reshape(1, -1), jnp.tile(gq, N_HEADS).reshape(1, -1), jnp.tile(gk, N_HEADS).reshape(1, -1),
      w0.reshape(1, -1), w2.astype(bf16), a0.reshape(1, -1), a2.astype(bf16), g2.astype(bf16),
      k_k.reshape(1, -1), k_a.reshape(1, -1), seg_ones, seg_bcast)
    return outs


def _suffix_matrix(tk):
    r = _iota((tk, 2 * tk), 0)
    c = _iota((tk, 2 * tk), 1)
    return jnp.where((c >= tk) | (r > c), 1.0, 0.0).astype(bf16)


def _sb_block(z, carry, suffix_mat, mask):
    tk = z.shape[1]
    sp = _softplus(z)
    log_1m = -sp
    log_beta = z - sp
    if mask is not None:
        log_1m = jnp.where(mask, log_1m, 0.0)
    st = _mm(log_1m, suffix_mat, NN, 3, 1)
    a = jnp.exp(log_beta + st[:, :tk] + carry)
    if mask is not None:
        a = jnp.where(mask, a, 0.0)
    return a, carry + st[:, tk:]


def _sb_prompt_body(q_ref, k_ref, v_ref, bias_ref, o_ref, *, tq):
    i = pl.program_id(2)
    suffix_mat = _suffix_matrix(tq)
    q = q_ref[0, 0].astype(bf16)
    bias = bias_ref[0]

    def block(j, carry, acc, mask):
        ks = pl.ds(pl.multiple_of(j * tq, tq), tq)
        z = _mm(q, k_ref[0, 0, ks, :].astype(bf16), NT) + bias
        a, carry = _sb_block(z, carry, suffix_mat, mask)
        acc = acc + _mm(a.astype(bf16), v_ref[0, 0, ks, :].astype(bf16))
        return carry, acc

    causal = _iota((tq, tq), 1) < _iota((tq, tq), 0)
    carry, acc = block(i, jnp.zeros((tq, tq), f32), jnp.zeros((tq, HEAD_DIM), f32), causal)

    def step(jj, ca):
        return block(i - 1 - jj, ca[0], ca[1], None)

    carry, acc = lax.fori_loop(0, i, step, (carry, acc))
    o_ref[0, 0] = acc


def _sb_prompt(q, k, v, bias, tq=128):
    b, h, t, d = q.shape
    bias_rows = jnp.broadcast_to(bias.reshape(h, 1, 1), (h, 1, LANES)).astype(f32)
    return pl.pallas_call(
        functools.partial(_sb_prompt_body, tq=tq),
        grid=(b, h, t // tq),
        in_specs=[pl.BlockSpec((1, 1, tq, d), lambda bi, hi, i: (bi, hi, i, 0)),
                  pl.BlockSpec((1, 1, t, d), lambda bi, hi, i: (bi, hi, 0, 0)),
                  pl.BlockSpec((1, 1, t, d), lambda bi, hi, i: (bi, hi, 0, 0)),
                  pl.BlockSpec((1, 1, LANES), lambda bi, hi, i: (hi, 0, 0))],
        out_specs=pl.BlockSpec((1, 1, tq, d), lambda bi, hi, i: (bi, hi, i, 0)),
        out_shape=jax.ShapeDtypeStruct((b, h, t, d), f32),
        compiler_params=_params("parallel", "parallel", "arbitrary"),
        name="sb_prompt",
    )(q, k, v, bias_rows)


def _sb_sample_body(pt_ref, q_ref, kn_ref, vn_ref, bias_ref, kc_ref, vc_ref, o_ref, qbd_scr, acc_scr, carry_scr,
                    *, n_new, n_pages):
    s = pl.program_id(1)
    rows = N_HEADS * n_new
    suffix_mat = _suffix_matrix(PAGE)
    bias = bias_ref[...]

    def attend(kb, vb, mask):
        z = _mm(qbd_scr[...], kb.astype(bf16), NT) + bias
        a, carry = _sb_block(z, carry_scr[...], suffix_mat, mask)
        carry_scr[...] = carry
        acc_scr[...] += _mm(a.astype(bf16), vb.astype(bf16))

    @pl.when(s == 0)
    def _():
        q = jnp.concatenate([q_ref[0]] * N_HEADS, axis=0)
        own = _iota((rows, D_GRP), 1) // HEAD_DIM == _iota((rows, D_GRP), 0) // n_new
        qbd_scr[...] = jnp.where(own, q, 0.0).astype(bf16)
        acc_scr[...] = jnp.zeros_like(acc_scr)
        carry_scr[...] = jnp.zeros_like(carry_scr)
        pad = jnp.zeros((PAGE - n_new, D_GRP), f32)
        kb = jnp.concatenate([kn_ref[0], pad], axis=0)
        vb = jnp.concatenate([vn_ref[0], pad], axis=0)
        causal = _iota((rows, PAGE), 1) < _iota((rows, PAGE), 0) % n_new
        attend(kb, vb, causal)

    @pl.when(s > 0)
    def _():
        attend(kc_ref[0, 0], vc_ref[0, 0], None)

    @pl.when(s == n_pages)
    def _():
        acc = acc_scr[...]
        lane_head = _iota((n_new, D_GRP), 1) // HEAD_DIM
        out = jnp.zeros((n_new, D_GRP), f32)
        for h in range(N_HEADS):
            out = out + jnp.where(lane_head == h, acc[h * n_new:(h + 1) * n_new, :], 0.0)
        o_ref[0] = out


def _sb_sample(q, k_new, v_new, bias, cache_k, cache_v, page_table, layer):
    nb, n_new, _ = q.shape
    n_pages = page_table.shape[1]
    rows = N_HEADS * n_new
    bias_rows = jnp.repeat(bias.astype(f32), n_new).reshape(rows, 1)

    def page(b, s, pt):
        return (layer, pt[b, jnp.minimum(n_pages - s, n_pages - 1)], 0, 0)

    tok = pl.BlockSpec((1, n_new, D_GRP), lambda b, s, pt: (b, 0, 0))
    grid_spec = pltpu.PrefetchScalarGridSpec(
        num_scalar_prefetch=1,
        grid=(nb, n_pages + 1),
        in_specs=[tok, tok, tok,
                  pl.BlockSpec((rows, 1), lambda b, s, pt: (0, 0)),
                  pl.BlockSpec((1, 1, PAGE, D_GRP), page),
                  pl.BlockSpec((1, 1, PAGE, D_GRP), page)],
        out_specs=tok,
        scratch_shapes=[pltpu.VMEM((rows, D_GRP), bf16), pltpu.VMEM((rows, D_GRP), f32),
                        pltpu.VMEM((rows, PAGE), f32)],
    )
    return pl.pallas_call(
        functools.partial(_sb_sample_body, n_new=n_new, n_pages=n_pages),
        grid_spec=grid_spec,
        out_shape=jax.ShapeDtypeStruct((nb, n_new, D_GRP), f32),
        compiler_params=_params("parallel", "arbitrary"),
        name="sb_sample",
    )(page_table, q, k_new, v_new, bias_rows, cache_k, cache_v)


def _unit_lower_inverse(a_strict, c):
    r = _iota((c, c), 0)
    col = _iota((c, c), 1)
    mm = lambda x, y: _mm(x, y, NN, 2, 2)
    base = 8
    p = jnp.where(r // base == col // base, a_strict, 0.0)
    t = jnp.where(r == col, 1.0, 0.0) + p
    for _ in range(2):
        p = mm(p, p)
        t = t + mm(t, p)
    s = base
    while s < c:
        join = (r // (2 * s) == col // (2 * s)) & (r // s != col // s)
        t = t + mm(t, mm(jnp.where(join, a_strict, 0.0), t))
        s *= 2
    return t


def _rwkv_body(*refs, c, g, n_chunks, has_init):
    if has_init:
        (r_ref, lw_ref, k_ref, v_ref, a_ref, b_ref, gate_ref, rk_ref, lng_ref, lnb_ref, s0_ref,
         o_ref, sout_ref, s_scr) = refs
    else:
        (r_ref, lw_ref, k_ref, v_ref, a_ref, b_ref, gate_ref, rk_ref, lng_ref, lnb_ref,
         o_ref, sout_ref, s_scr) = refs
    tstep = pl.program_id(2)

    @pl.when(tstep == 0)
    def _():
        s_scr[...] = s0_ref[0] if has_init else jnp.zeros_like(s_scr)

    row = _iota((c, c), 0)
    col = _iota((c, c), 1)
    incl = jnp.where(row >= col, 1.0, 0.0).astype(bf16)
    row2 = _iota((2 * c, 2 * c), 0)
    col2 = _iota((2 * c, 2 * c), 1)
    lower = (col2 % c) < jnp.where(row2 < c, row2, row2 - c + 1)
    mm = lambda x, y, dn=NN: _mm(x, y, dn, 2, 2)

    def chunk(ci, carry):
        ts = pl.ds(pl.multiple_of(ci * c, c), c)
        for gi in range(g):
            r = r_ref[0, gi, ts, :]
            lw = lw_ref[0, gi, ts, :]
            k = k_ref[0, gi, ts, :]
            v = v_ref[0, gi, ts, :]
            a = a_ref[0, gi, ts, :]
            b = b_ref[0, gi, ts, :]
            s0 = s_scr[gi]
            cum = _mm(incl, lw, NN, 1, 3)
            grow = jnp.exp(-cum)
            left = jnp.concatenate([a * jnp.exp(cum - lw), r * jnp.exp(cum)], axis=0)
            right = jnp.concatenate([b * grow, k * grow], axis=0)
            gram = jnp.where(lower, mm(left, right, NT), 0.0)
            from_state = mm(left, s0, NT)
            t_inv = _unit_lower_inverse(gram[:c, :c], c)
            u = mm(t_inv, from_state[:c] + mm(gram[:c, c:], v))
            uv = jnp.concatenate([u, v], axis=0)
            y = from_state[c:] + mm(gram[c:], uv)
            s_new = (s0 + mm(uv, right, TN)) * jnp.exp(cum[c - 1:c, :])
            s_scr[gi] = s_new

            mu = jnp.mean(y, axis=-1, keepdims=True)
            yc = y - mu
            var = jnp.mean(yc * yc, axis=-1, keepdims=True)
            yn = yc * lax.rsqrt(var + LN_X_EPS) * lng_ref[gi] + lnb_ref[gi]
            bonus = jnp.sum(r * k * rk_ref[gi], axis=-1, keepdims=True) * v
            o_ref[0, gi, ts, :] = (yn + bonus) * gate_ref[0, gi, ts, :]
        return carry

    lax.fori_loop(0, n_chunks, chunk, 0)

    @pl.when(tstep == pl.num_programs(2) - 1)
    def _():
        sout_ref[0] = s_scr[...]


def _rwkv_scan(r, lw, k, v, a, b, gate, r_k, ln_g, ln_b, s0, c, g, t_blk):
    nb, h, t, d = r.shape
    seq = pl.BlockSpec((1, g, t_blk, d), lambda bi, hi, ti: (bi, hi, ti, 0))
    per_head = pl.BlockSpec((g, 1, d), lambda bi, hi, ti: (hi, 0, 0))
    state = pl.BlockSpec((1, g, d, d), lambda bi, hi, ti: (bi, hi, 0, 0))
    has_init = s0 is not None
    args = [r, lw, k, v, a, b, gate, r_k.reshape(h, 1, d), ln_g.reshape(h, 1, d), ln_b.reshape(h, 1, d)]
    in_specs = [seq] * 7 + [per_head] * 3
    if has_init:
        args.append(s0)
        in_specs.append(state)
    return pl.pallas_call(
        functools.partial(_rwkv_body, c=c, g=g, n_chunks=t_blk // c, has_init=has_init),
        grid=(nb, h // g, t // t_blk),
        in_specs=in_specs,
        out_specs=[seq, state],
        out_shape=[jax.ShapeDtypeStruct((nb, h, t, d), f32), jax.ShapeDtypeStruct((nb, h, d, d), f32)],
        scratch_shapes=[pltpu.VMEM((g, d, d), f32)],
        compiler_params=_params("parallel", "parallel", "arbitrary"),
        name="rwkv_scan",
    )(*args)


def _out_proj_body(x_ref, osb_ref, orw_ref, scale_ref, w_ref, o_ref):
    o_sb = (osb_ref[...] * scale_ref[...]).astype(bf16)
    acc = jnp.dot(o_sb, w_ref[0:D_GRP, :], preferred_element_type=f32)
    acc = acc + jnp.dot(orw_ref[...].astype(bf16), w_ref[D_GRP:, :], preferred_element_type=f32)
    o_ref[...] = x_ref[...] + acc


def _out_proj(x, o_sb, o_rw, scale, w_bf16, tm=512):
    t, d = x.shape
    return pl.pallas_call(
        _out_proj_body,
        grid=(t // tm,),
        in_specs=[pl.BlockSpec((tm, d), lambda i: (i, 0)),
                  pl.BlockSpec((tm, D_GRP), lambda i: (i, 0)),
                  pl.BlockSpec((tm, D_GRP), lambda i: (i, 0)),
                  pl.BlockSpec((1, D_GRP), lambda i: (0, 0)),
                  pl.BlockSpec(w_bf16.shape, lambda i: (0, 0))],
        out_specs=pl.BlockSpec((tm, d), lambda i: (i, 0)),
        out_shape=jax.ShapeDtypeStruct((t, d), f32),
        compiler_params=_params("parallel"),
        name="out_proj",
    )(x, o_sb, o_rw, scale.reshape(1, -1), w_bf16)


def _rms_rows(x, g):
    return x * lax.rsqrt(jnp.mean(x * x, axis=-1, keepdims=True) + RMS_EPS) * g


def _ffn_body(x_ref, g_ref, wg_ref, wu_ref, wd_ref, o_ref, h_scr):
    f = pl.program_id(1)

    @pl.when(f == 0)
    def _():
        x = x_ref[...]
        h_scr[...] = _rms_rows(x, g_ref[...]).astype(bf16)
        o_ref[...] = x

    h = h_scr[...]
    act = jax.nn.silu(jnp.dot(h, wg_ref[...], preferred_element_type=f32)) * jnp.dot(h, wu_ref[...], preferred_element_type=f32)
    o_ref[...] += jnp.dot(act.astype(bf16), wd_ref[...], preferred_element_type=f32)


def _ffn(x, g, wg, wu, wd, tm=512, tf=512):
    t, d = x.shape
    dff = wg.shape[1]
    return pl.pallas_call(
        _ffn_body,
        grid=(t // tm, dff // tf),
        in_specs=[pl.BlockSpec((tm, d), lambda i, f: (i, 0)),
                  pl.BlockSpec((1, d), lambda i, f: (0, 0)),
                  pl.BlockSpec((d, tf), lambda i, f: (0, f)),
                  pl.BlockSpec((d, tf), lambda i, f: (0, f)),
                  pl.BlockSpec((tf, d), lambda i, f: (f, 0))],
        out_specs=pl.BlockSpec((tm, d), lambda i, f: (i, 0)),
        out_shape=jax.ShapeDtypeStruct((t, d), f32),
        scratch_shapes=[pltpu.VMEM((tm, d), bf16)],
        compiler_params=_params("parallel", "arbitrary"),
        name="ffn_dense",
    )(x, g.reshape(1, d), wg, wu, wd)


def _top2_gates(logits):
    lane = _iota(logits.shape, 1)
    valid = lane < N_EXPERTS
    m = jnp.max(jnp.where(valid, logits, -jnp.inf), axis=-1, keepdims=True)
    e = jnp.where(valid, jnp.exp(logits - m), 0.0)
    p = e / jnp.sum(e, axis=-1, keepdims=True)
    p1 = jnp.max(p, axis=-1, keepdims=True)
    i1 = jnp.min(jnp.where(valid & (p == p1), lane, LANES), axis=-1, keepdims=True)
    rest = jnp.where(valid & (lane != i1), p, -1.0)
    p2 = jnp.max(rest, axis=-1, keepdims=True)
    i2 = jnp.min(jnp.where(rest == p2, lane, LANES), axis=-1, keepdims=True)
    tot = p1 + p2
    return jnp.where(lane == i1, p1 / tot, jnp.where(lane == i2, p2 / tot, 0.0))


def _moe_body(x_ref, g_ref, router_ref, wg_ref, wu_ref, wd_ref, o_ref, h_scr, gates_scr):
    e = pl.program_id(1)
    f = pl.program_id(2)

    @pl.when((e == 0) & (f == 0))
    def _():
        x = x_ref[...]
        hn = _rms_rows(x, g_ref[...])
        h_scr[...] = hn.astype(bf16)
        gates_scr[...] = _top2_gates(_mm(hn, router_ref[...], NN, 3, 3))
        o_ref[...] = x

    h = h_scr[...]
    gates = gates_scr[...]
    gate = jnp.sum(jnp.where(_iota(gates.shape, 1) == e, gates, 0.0), axis=-1, keepdims=True)
    act = jax.nn.silu(jnp.dot(h, wg_ref[0], preferred_element_type=f32)) * jnp.dot(h, wu_ref[0], preferred_element_type=f32)
    o_ref[...] += jnp.dot((act * gate).astype(bf16), wd_ref[0], preferred_element_type=f32)


def _moe(x, g, router, wg, wu, wd, tm=512, tf=256):
    t, d = x.shape
    ne, _, de = wg.shape
    router_pad = jnp.zeros((d, LANES), f32).at[:, :ne].set(router)
    return pl.pallas_call(
        _moe_body,
        grid=(t // tm, ne, de // tf),
        in_specs=[pl.BlockSpec((tm, d), lambda i, e, f: (i, 0)),
                  pl.BlockSpec((1, d), lambda i, e, f: (0, 0)),
                  pl.BlockSpec((d, LANES), lambda i, e, f: (0, 0)),
                  pl.BlockSpec((1, d, tf), lambda i, e, f: (e, 0, f)),
                  pl.BlockSpec((1, d, tf), lambda i, e, f: (e, 0, f)),
                  pl.BlockSpec((1, tf, d), lambda i, e, f: (e, f, 0))],
        out_specs=pl.BlockSpec((tm, d), lambda i, e, f: (i, 0)),
        out_shape=jax.ShapeDtypeStruct((t, d), f32),
        scratch_shapes=[pltpu.VMEM((tm, d), bf16), pltpu.VMEM((tm, LANES), f32)],
        compiler_params=_params("parallel", "arbitrary", "arbitrary"),
        name="moe",
    )(x, g.reshape(1, d), router_pad, wg, wu, wd)


def _to_heads(x, nb):
    return x.reshape(nb, -1, N_HEADS, HEAD_DIM).transpose(0, 2, 1, 3)


def _from_heads(x):
    nb, h, t, d = x.shape
    return x.transpose(0, 2, 1, 3).reshape(nb * t, h * d)


def kernel(x_prompt, x_sample, cache_k, cache_v, page_table, state_wkv, state_shift, norm_mix_g, w_in, sb_q_norm_g, sb_k_norm_g, sb_logit_bias, sb_out_scale, rw_shift_mu, rw_w0, rw_w2, rw_a0, rw_a2, rw_g2, rw_k_k, rw_k_a, rw_r_k, rw_ln_g, rw_ln_b, w_out, norm_ffn_g, ffn_w_gate, ffn_w_up, ffn_w_down, moe_router, moe_w_gate, moe_w_up, moe_w_down):
    bp, tp, d = x_prompt.shape
    nb, ts, _ = x_sample.shape
    depth = w_in.shape[0]
    n_p = bp * tp
    x = jnp.concatenate([x_prompt.reshape(n_p, d), x_sample.reshape(nb * ts, d)], axis=0)
    cache_k = cache_k.reshape(cache_k.shape[:3] + (D_GRP,))
    cache_v = cache_v.reshape(cache_v.shape[:3] + (D_GRP,))
    rw0 = 3 * D_GRP

    kp, vp, ksm, vsm, wp, wsm, sp, ssm = [], [], [], [], [], [], [], []
    for l in range(depth):
        proj = _norm_proj(x, norm_mix_g[l], w_in[l].astype(bf16))
        p_rw_p = proj[:n_p, rw0:].reshape(bp, tp, N_SHIFT)
        p_rw_s = proj[n_p:, rw0:].reshape(nb, ts, N_SHIFT)
        prev = jnp.concatenate([
            jnp.concatenate([jnp.zeros((bp, 1, N_SHIFT), f32), p_rw_p[:, :-1]], axis=1).reshape(n_p, N_SHIFT),
            jnp.concatenate([state_shift[l][:, None, :], p_rw_s[:, :-1]], axis=1).reshape(nb * ts, N_SHIFT)], axis=0)
        q, k, r, lw, k2, v_rw, a, b, gate = _mix_prep(
            proj, prev, rw_shift_mu[l], sb_q_norm_g[l], sb_k_norm_g[l], rw_w0[l], rw_w2[l], rw_a0[l], rw_a2[l],
            rw_g2[l], rw_k_k[l], rw_k_a[l])
        v = proj[:, 2 * D_GRP:3 * D_GRP]

        o_sb_p = _from_heads(_sb_prompt(_to_heads(q[:n_p], bp), _to_heads(k[:n_p], bp), _to_heads(v[:n_p], bp),
                                        sb_logit_bias[l]))
        o_sb_s = _sb_sample(q[n_p:].reshape(nb, ts, D_GRP), k[n_p:].reshape(nb, ts, D_GRP),
                            v[n_p:].reshape(nb, ts, D_GRP), sb_logit_bias[l], cache_k, cache_v, page_table, l)
        o_sb = jnp.concatenate([o_sb_p, o_sb_s.reshape(nb * ts, D_GRP)], axis=0)

        rw_in = (r, lw, k2, v_rw, a, b, gate)
        o_rw_p, wkv_p = _rwkv_scan(*[_to_heads(z[:n_p], bp) for z in rw_in], rw_r_k[l], rw_ln_g[l].reshape(N_HEADS, HEAD_DIM),
                                   rw_ln_b[l].reshape(N_HEADS, HEAD_DIM), None, c=64, g=4, t_blk=512)
        o_rw_s, wkv_s = _rwkv_scan(*[_to_heads(z[n_p:], nb) for z in rw_in], rw_r_k[l], rw_ln_g[l].reshape(N_HEADS, HEAD_DIM),
                                   rw_ln_b[l].reshape(N_HEADS, HEAD_DIM), state_wkv[l], c=ts, g=N_HEADS, t_blk=ts)
        o_rw = jnp.concatenate([_from_heads(o_rw_p), _from_heads(o_rw_s)], axis=0)

        x = _out_proj(x, o_sb, o_rw, sb_out_scale[l], w_out[l].astype(bf16))
        i = l // 2
        if l % 2 == 0:
            x = _ffn(x, norm_ffn_g[l], ffn_w_gate[i].astype(bf16), ffn_w_up[i].astype(bf16), ffn_w_down[i].astype(bf16))
        else:
            x = _moe(x, norm_ffn_g[l], moe_router[i], moe_w_gate[i].astype(bf16), moe_w_up[i].astype(bf16),
                     moe_w_down[i].astype(bf16))

        kp.append(k[:n_p].reshape(bp, tp, N_HEADS, HEAD_DIM))
        vp.append(v[:n_p].reshape(bp, tp, N_HEADS, HEAD_DIM))
        ksm.append(k[n_p:].reshape(nb, ts, N_HEADS, HEAD_DIM))
        vsm.append(v[n_p:].reshape(nb, ts, N_HEADS, HEAD_DIM))
        wp.append(wkv_p)
        wsm.append(wkv_s)
        sp.append(p_rw_p[:, -1])
        ssm.append(p_rw_s[:, -1])

    return (x[:n_p].reshape(bp, tp, d), x[n_p:].reshape(nb, ts, d), jnp.stack(kp), jnp.stack(vp), jnp.stack(ksm),
            jnp.stack(vsm), jnp.stack(wp), jnp.stack(wsm), jnp.stack(sp), jnp.stack(ssm))
```

```python
import functools

import jax
import jax.numpy as jnp
from jax import lax
from jax.experimental import pallas as pl
from jax.experimental.pallas import tpu as pltpu

f32 = jnp.float32
bf16 = jnp.bfloat16

D_MODEL = 2048
HEAD_DIM = 64
N_HEADS = 16
D_GRP = N_HEADS * HEAD_DIM
R_DECAY = 64
R_AAA = 64
R_GATE = 128
N_SHIFT = 3 * D_GRP + R_DECAY + R_AAA + R_GATE
N_IN = 3 * D_GRP + N_SHIFT
N_EXPERTS = 8
PAGE = 128
RMS_EPS = 1e-6
LN_X_EPS = 64e-5
QK_SCALE = HEAD_DIM ** -0.5

LANES = 128
SUBLANES = 8
VMEM_LIMIT = 56 * 1024 * 1024

NN = (((1,), (0,)), ((), ()))
NT = (((1,), (1,)), ((), ()))
TN = (((0,), (0,)), ((), ()))


def _params(*sem):
    return pltpu.CompilerParams(dimension_semantics=sem, vmem_limit_bytes=VMEM_LIMIT)


def _split(x, n):
    out, r = [], x
    for i in range(n):
        h = r.astype(bf16)
        out.append(h)
        if i + 1 < n:
            r = r - h.astype(f32)
    return out


def _mm(a, b, dn=NN, pa=1, pb=1):
    sa = [a] if a.dtype == bf16 else _split(a, pa)
    sb = [b] if b.dtype == bf16 else _split(b, pb)
    order = max(len(sa), len(sb))
    acc = None
    for i, ai in enumerate(sa):
        for j, bj in enumerate(sb):
            if i + j < order:
                t = lax.dot_general(ai, bj, dn, preferred_element_type=f32)
                acc = t if acc is None else acc + t
    return acc


def _softplus(z):
    return jnp.maximum(z, 0.0) + jnp.log1p(jnp.exp(-jnp.abs(z)))


def _iota(shape, dim):
    return lax.broadcasted_iota(jnp.int32, shape, dim)


def _head(x, h):
    return x[:, h * HEAD_DIM:(h + 1) * HEAD_DIM]


def _norm_proj_body(x_ref, g_ref, w_ref, o_ref, h_scr):
    @pl.when(pl.program_id(1) == 0)
    def _():
        x = x_ref[...]
        y = x * lax.rsqrt(jnp.mean(x * x, axis=-1, keepdims=True) + RMS_EPS)
        h_scr[...] = (y * g_ref[...]).astype(bf16)

    o_ref[...] = jnp.dot(h_scr[...], w_ref[...], preferred_element_type=f32)


def _norm_proj(x, g, w_bf16, tm=512, tn=640):
    t, d = x.shape
    n = w_bf16.shape[1]
    return pl.pallas_call(
        _norm_proj_body,
        grid=(t // tm, n // tn),
        in_specs=[pl.BlockSpec((tm, d), lambda i, j: (i, 0)),
                  pl.BlockSpec((1, d), lambda i, j: (0, 0)),
                  pl.BlockSpec((d, tn), lambda i, j: (0, j))],
        out_specs=pl.BlockSpec((tm, tn), lambda i, j: (i, j)),
        out_shape=jax.ShapeDtypeStruct((t, n), f32),
        scratch_shapes=[pltpu.VMEM((tm, d), bf16)],
        compiler_params=_params("parallel", "arbitrary"),
        name="norm_proj",
    )(x, g.reshape(1, d), w_bf16)


def _mix_prep_body(proj_ref, prev_ref, mu_ref, gq_ref, gk_ref, w0_ref, w2_ref, a0_ref, a2_ref, g2_ref,
                   kk_ref, ka_ref, r1_ref, r2_ref,
                   q_ref, k_ref, r_ref, lw_ref, k2_ref, v_ref, a_ref, b_ref, g_ref):
    seg_ones = r1_ref[...]
    seg_bcast = r2_ref[...]

    def head_sum(x):
        return _mm(x, seg_ones, NN, 3, 1)

    def head_spread(s):
        return _mm(s, seg_bcast, NN, 3, 1)

    q = proj_ref[:, 0:D_GRP]
    qn = q * head_spread(lax.rsqrt(head_sum(q * q) * (1.0 / HEAD_DIM) + RMS_EPS))
    q_ref[...] = qn * gq_ref[...] * QK_SCALE
    k = proj_ref[:, D_GRP:2 * D_GRP]
    kn = k * head_spread(lax.rsqrt(head_sum(k * k) * (1.0 / HEAD_DIM) + RMS_EPS))
    k_ref[...] = kn * gk_ref[...]

    base = 3 * D_GRP

    def shifted(lo, hi):
        p = proj_ref[:, base + lo:base + hi]
        return p + (prev_ref[:, lo:hi] - p) * mu_ref[:, lo:hi]

    r = shifted(0, D_GRP)
    kr = shifted(D_GRP, 2 * D_GRP)
    v = shifted(2 * D_GRP, 3 * D_GRP)
    tail = shifted(3 * D_GRP, N_SHIFT)
    wd = tail[:, 0:R_DECAY]
    ad = tail[:, R_DECAY:R_DECAY + R_AAA]
    gd = tail[:, R_DECAY + R_AAA:]

    w_pre = w0_ref[...] + _mm(jnp.tanh(wd), w2_ref[...])
    w_log = -_softplus(-w_pre) - 0.5
    lw_ref[...] = -jnp.exp(w_log)
    a_lr = jax.nn.sigmoid(a0_ref[...] + _mm(ad, a2_ref[...]))
    g_ref[...] = _mm(jax.nn.sigmoid(gd), g2_ref[...])

    kk = kr * kk_ref[...]
    kk = kk * head_spread(lax.rsqrt(jnp.maximum(head_sum(kk * kk), 1e-24)))
    r_ref[...] = r
    k2_ref[...] = kr * (1.0 + (a_lr - 1.0) * ka_ref[...])
    v_ref[...] = v
    a_ref[...] = -kk
    b_ref[...] = kk * a_lr


def _mix_prep(proj, prev, mu, gq, gk, w0, w2, a0, a2, g2, k_k, k_a, tm=128):
    t = proj.shape[0]
    heads = _iota((D_GRP, LANES), 0) // HEAD_DIM
    seg_ones = (heads == _iota((D_GRP, LANES), 1)).astype(bf16)
    seg_bcast = seg_ones.T
    row = lambda n: pl.BlockSpec((1, n), lambda i: (0, 0))
    full = lambda a: pl.BlockSpec(a.shape, lambda i: (0, 0))
    tok = lambda n: pl.BlockSpec((tm, n), lambda i: (i, 0))
    outs = pl.pallas_call(
        _mix_prep_body,
        grid=(t // tm,),
        in_specs=[tok(N_IN), tok(N_SHIFT), row(N_SHIFT), row(D_GRP), row(D_GRP), row(D_GRP), full(w2), row(D_GRP),
                  full(a2), full(g2), row(D_GRP), row(D_GRP), full(seg_ones), full(seg_bcast)],
        out_specs=[tok(D_GRP)] * 9,
        out_shape=[jax.ShapeDtypeStruct((t, D_GRP), f32)] * 9,
        compiler_params=_params("parallel"),
        name="mix_prep",
    )(proj, prev, mu.reshape(1, -1), jnp.tile(gq, N_HEADS).reshape(1, -1), jnp.tile(gk, N_HEADS).reshape(1, -1),
      w0.reshape(1, -1), w2.astype(bf16), a0.reshape(1, -1), a2.astype(bf16), g2.astype(bf16),
      k_k.reshape(1, -1), k_a.reshape(1, -1), seg_ones, seg_bcast)
    return outs


def _suffix_matrix(tk):
    r = _iota((2 * tk, 2 * tk), 0) % tk
    c = _iota((2 * tk, 2 * tk), 1)
    return jnp.where((c >= tk) | (r > c), 1.0, 0.0).astype(bf16)


def _sb_scores(z, suffix_mat, mask):
    tk = z.shape[1]
    sp = jnp.maximum(z, 0.0) + jnp.log(1.0 + jnp.exp(-jnp.abs(z)))
    log_1m = -sp
    if mask is not None:
        log_1m = jnp.where(mask, log_1m, 0.0)
    hi = log_1m.astype(bf16)
    lo = (log_1m - hi.astype(f32)).astype(bf16)
    st = jnp.dot(jnp.concatenate([hi, lo], axis=1), suffix_mat, preferred_element_type=f32)
    return (z - sp) + st[:, :tk], st[:, tk:]


def _sb_prompt_body(q_ref, k_ref, v_ref, bias_ref, sfx_ref, o_ref, carry_scr, acc_scr, *, tq, tk):
    i = pl.program_id(2)
    ratio = tq // tk
    lane = _iota((tq, LANES), 1)
    first = lane < HEAD_DIM
    q2 = q_ref[...]
    qh = [jnp.where(first, q2, 0.0).astype(bf16), jnp.where(first, 0.0, q2).astype(bf16)]
    bias = [bias_ref[0, 0:1, :], bias_ref[0, 1:2, :]]
    sfx = sfx_ref[...]
    carry_scr[...] = jnp.zeros_like(carry_scr)
    acc_scr[...] = jnp.zeros_like(acc_scr)
    row = _iota((tq, tk), 0)
    col = _iota((tq, tk), 1)

    def block(j, masked):
        ks = pl.ds(pl.multiple_of(j * tk, tk), tk)
        kb = k_ref[ks, :].astype(bf16)
        vb = v_ref[ks, :].astype(bf16)
        mask = (j * tk + col < i * tq + row) if masked else None
        zs = [_mm(qh[h], kb, NT) + bias[h] for h in range(2)]
        sc = [_sb_scores(z, sfx, mask) for z in zs]
        ws = []
        for h in range(2):
            carry = carry_scr[h]
            a = jnp.exp(sc[h][0] + carry)
            if masked:
                a = jnp.where(mask, a, 0.0)
            carry_scr[h] = carry + sc[h][1]
            ws.append(a.astype(bf16))
        os_ = [_mm(w, vb) for w in ws]
        acc_scr[...] += jnp.where(first, os_[0], os_[1])

    for u in range(ratio):
        block((i + 1) * ratio - 1 - u, True)

    def step(jj, c):
        block(i * ratio - 1 - jj, False)
        return c

    lax.fori_loop(0, i * ratio, step, 0)
    o_ref[...] = acc_scr[...]


def _sb_prompt(q, k, proj, bias, nb, t, tq=256, tk=PAGE):
    pairs = N_HEADS // 2
    nq = t // tq
    bias_rows = jnp.broadcast_to(bias.astype(f32).reshape(pairs, 2, 1), (pairs, 2, LANES))
    v_blk0 = 2 * D_GRP // LANES
    return pl.pallas_call(
        functools.partial(_sb_prompt_body, tq=tq, tk=tk),
        grid=(nb, pairs, nq),
        in_specs=[pl.BlockSpec((tq, LANES), lambda b, p, i: (b * nq + i, p)),
                  pl.BlockSpec((t, LANES), lambda b, p, i: (b, p)),
                  pl.BlockSpec((t, LANES), lambda b, p, i: (b, v_blk0 + p)),
                  pl.BlockSpec((1, 2, LANES), lambda b, p, i: (p, 0, 0)),
                  pl.BlockSpec((2 * tk, 2 * tk), lambda b, p, i: (0, 0))],
        out_specs=pl.BlockSpec((tq, LANES), lambda b, p, i: (b * nq + i, p)),
        out_shape=jax.ShapeDtypeStruct((nb * t, D_GRP), f32),
        scratch_shapes=[pltpu.VMEM((2, tq, tk), f32), pltpu.VMEM((tq, LANES), f32)],
        compiler_params=_params("parallel", "parallel", "arbitrary"),
        name="sb_prompt",
    )(q, k, proj, bias_rows, _suffix_matrix(tk))


def _sb_sample_body(pt_ref, q_ref, kn_ref, vn_ref, bias_ref, sfx_ref, *refs, n_new, pp):
    kc_refs, vc_refs = refs[:pp], refs[pp:2 * pp]
    o_ref, q_scr, acc_scr, carry_scr = refs[2 * pp:]
    s = pl.program_id(1)
    rows = N_HEADS * n_new
    heads = range(N_HEADS)
    sfx = sfx_ref[...]
    bias = bias_ref[...]
    hrows = lambda x, h: x[h * n_new:(h + 1) * n_new]

    @pl.when(s == 0)
    def _():
        q = q_ref[...]
        for h in heads:
            q_scr[h] = _head(q, h)
        pad = jnp.zeros((PAGE - n_new, HEAD_DIM), f32)
        kn, vn = kn_ref[...], vn_ref[...]
        zs = [_mm(_head(q, h).astype(bf16), jnp.concatenate([_head(kn, h), pad], axis=0).astype(bf16), NT)
              for h in heads]
        causal = _iota((rows, PAGE), 1) < _iota((rows, PAGE), 0) % n_new
        lbs, tot = _sb_scores(jnp.concatenate(zs, axis=0) + bias, sfx, causal)
        a = jnp.where(causal, jnp.exp(lbs), 0.0)
        os_ = [_mm(hrows(a, h).astype(bf16), jnp.concatenate([_head(vn, h), pad], axis=0).astype(bf16))
               for h in heads]
        acc_scr[...] = jnp.concatenate(os_, axis=0)
        carry_scr[...] = tot

    qb = [q_scr[h].astype(bf16) for h in heads]
    zs = [jnp.concatenate([_mm(qb[h], kc_refs[u][0, 0, h].astype(bf16)) for h in heads], axis=0) + bias
          for u in range(pp)]
    sc = [_sb_scores(z, sfx, None) for z in zs]
    carry = carry_scr[...]
    ws = []
    for u in range(pp):
        ws.append(jnp.exp(sc[u][0] + carry))
        carry = carry + sc[u][1]
    carry_scr[...] = carry
    os_ = [[_mm(hrows(ws[u], h).astype(bf16), vc_refs[u][0, 0, h].astype(bf16), NT) for h in heads]
           for u in range(pp)]
    upd = [functools.reduce(lambda x, y: x + y, [os_[u][h] for u in range(pp)]) for h in heads]
    acc_scr[...] += jnp.concatenate(upd, axis=0)

    @pl.when(s == pl.num_programs(1) - 1)
    def _():
        acc = acc_scr[...]
        for h in heads:
            o_ref[0, h] = hrows(acc, h)


def _sb_sample(q, k, proj, bias, cache_k_t, cache_v_t, page_table, layer, row0, n_new, pp=4):
    nb, n_pages = page_table.shape
    rows = N_HEADS * n_new
    bias_rows = jnp.repeat(bias.astype(f32), n_new).reshape(rows, 1)
    blk0 = row0 // n_new
    v_blk0 = 2 * D_GRP // D_GRP

    def page(u):
        return lambda b, s, pt: (layer, pt[b, n_pages - 1 - (s * pp + u)], 0, 0, 0)

    tok = lambda cb: pl.BlockSpec((n_new, D_GRP), lambda b, s, pt: (blk0 + b, cb))
    page_specs = [pl.BlockSpec((1, 1, N_HEADS, HEAD_DIM, PAGE), page(u)) for u in range(pp)]
    grid_spec = pltpu.PrefetchScalarGridSpec(
        num_scalar_prefetch=1,
        grid=(nb, n_pages // pp),
        in_specs=[tok(0), tok(0), tok(v_blk0),
                  pl.BlockSpec((rows, 1), lambda b, s, pt: (0, 0)),
                  pl.BlockSpec((2 * PAGE, 2 * PAGE), lambda b, s, pt: (0, 0))] + page_specs + page_specs,
        out_specs=pl.BlockSpec((1, N_HEADS, n_new, HEAD_DIM), lambda b, s, pt: (b, 0, 0, 0)),
        scratch_shapes=[pltpu.VMEM((N_HEADS, n_new, HEAD_DIM), f32), pltpu.VMEM((rows, HEAD_DIM), f32),
                        pltpu.VMEM((rows, PAGE), f32)],
    )
    return pl.pallas_call(
        functools.partial(_sb_sample_body, n_new=n_new, pp=pp),
        grid_spec=grid_spec,
        out_shape=jax.ShapeDtypeStruct((nb, N_HEADS, n_new, HEAD_DIM), f32),
        compiler_params=_params("parallel", "arbitrary"),
        name="sb_sample",
    )(page_table, q, k, proj, bias_rows, _suffix_matrix(PAGE),
      *([cache_k_t] * pp), *([cache_v_t] * pp))


def _unit_lower_inverses(a_strict, c):
    r = _iota((c, c), 0)
    col = _iota((c, c), 1)
    mm = lambda x, y: _mm(x, y, NN, 2, 2)
    base = SUBLANES
    ps = [jnp.where(r // base == col // base, a, 0.0) for a in a_strict]
    eye = jnp.where(r == col, 1.0, 0.0)
    ts = [eye + p for p in ps]
    for _ in range(2):
        ps = [mm(p, p) for p in ps]
        ts = [t + mm(t, p) for t, p in zip(ts, ps)]
    s = base
    while s < c:
        join = (r // (2 * s) == col // (2 * s)) & (r // s != col // s)
        low = [mm(jnp.where(join, a, 0.0), t) for a, t in zip(a_strict, ts)]
        ts = [t + mm(t, x) for t, x in zip(ts, low)]
        s *= 2
    return ts


def _rwkv_body(*refs, c, n_chunks, has_init):
    if has_init:
        (r_ref, lw_ref, k_ref, v_ref, a_ref, b_ref, gate_ref, rk_ref, lng_ref, lnb_ref, s0_ref,
         o_ref, sout_ref, s_scr) = refs
    else:
        (r_ref, lw_ref, k_ref, v_ref, a_ref, b_ref, gate_ref, rk_ref, lng_ref, lnb_ref,
         o_ref, sout_ref, s_scr) = refs
    tstep = pl.program_id(1)
    heads = range(N_HEADS)

    @pl.when(tstep == 0)
    def _():
        s_scr[...] = s0_ref[0] if has_init else jnp.zeros_like(s_scr)

    row = _iota((c, c), 0)
    col = _iota((c, c), 1)
    incl = jnp.where(row >= col, 1.0, 0.0).astype(bf16)
    row2 = _iota((2 * c, 2 * c), 0)
    col2 = _iota((2 * c, 2 * c), 1)
    lower = (col2 % c) < jnp.where(row2 < c, row2, row2 - c + 1)
    mm = lambda x, y, dn=NN: _mm(x, y, dn, 2, 2)
    stack = lambda x, y: jnp.concatenate([x, y], axis=0)

    def chunk(ci, carry):
        ts = pl.ds(pl.multiple_of(ci * c, c), c)
        r, lw, k, v = r_ref[ts, :], lw_ref[ts, :], k_ref[ts, :], v_ref[ts, :]
        cum = _mm(incl, lw, NN, 1, 3)
        grow = jnp.exp(-cum)
        a_t = a_ref[ts, :] * jnp.exp(cum - lw)
        r_t = r * jnp.exp(cum)
        b_t = b_ref[ts, :] * grow
        k_t = k * grow
        end = jnp.exp(cum[c - 1:c, :])
        left = [stack(_head(a_t, h), _head(r_t, h)) for h in heads]
        right = [stack(_head(b_t, h), _head(k_t, h)) for h in heads]
        vs = [_head(v, h) for h in heads]
        s0 = [s_scr[h] for h in heads]
        gram = [jnp.where(lower, mm(l, rt, NT), 0.0) for l, rt in zip(left, right)]
        from_state = [mm(l, s, NT) for l, s in zip(left, s0)]
        t_inv = _unit_lower_inverses([g[:c, :c] for g in gram], c)
        akv = [mm(g[:c, c:], x) for g, x in zip(gram, vs)]
        us = [mm(t, f[:c] + x) for t, f, x in zip(t_inv, from_state, akv)]
        uv = [stack(u, x) for u, x in zip(us, vs)]
        ys = [f[c:] + mm(g[c:], x) for f, g, x in zip(from_state, gram, uv)]
        s_new = [(s + mm(x, rt, TN)) * _head(end, h) for h, (s, x, rt) in enumerate(zip(s0, uv, right))]
        for h in heads:
            s_scr[h] = s_new[h]
            y = ys[h]
            mu = jnp.mean(y, axis=-1, keepdims=True)
            yc = y - mu
            var = jnp.mean(yc * yc, axis=-1, keepdims=True)
            yn = yc * lax.rsqrt(var + LN_X_EPS) * _head(lng_ref, h) + _head(lnb_ref, h)
            bonus = jnp.sum(_head(r, h) * _head(k, h) * _head(rk_ref, h), axis=-1, keepdims=True) * vs[h]
            o_ref[ts, h * HEAD_DIM:(h + 1) * HEAD_DIM] = (yn + bonus) * gate_ref[ts, h * HEAD_DIM:(h + 1) * HEAD_DIM]
        return carry

    lax.fori_loop(0, n_chunks, chunk, 0)

    @pl.when(tstep == pl.num_programs(1) - 1)
    def _():
        sout_ref[0] = s_scr[...]


def _rwkv_scan(rw_in, r_k, ln_g, ln_b, s0, nb, t, row0, c, t_blk):
    nt = t // t_blk
    blk0 = row0 // t_blk
    seq = pl.BlockSpec((t_blk, D_GRP), lambda b, ti: (blk0 + b * nt + ti, 0))
    per_lane = pl.BlockSpec((1, D_GRP), lambda b, ti: (0, 0))
    state = pl.BlockSpec((1, N_HEADS, HEAD_DIM, HEAD_DIM), lambda b, ti: (b, 0, 0, 0))
    has_init = s0 is not None
    args = list(rw_in) + [r_k.reshape(1, D_GRP), ln_g.reshape(1, D_GRP), ln_b.reshape(1, D_GRP)]
    in_specs = [seq] * 7 + [per_lane] * 3
    if has_init:
        args.append(s0)
        in_specs.append(state)
    return pl.pallas_call(
        functools.partial(_rwkv_body, c=c, n_chunks=t_blk // c, has_init=has_init),
        grid=(nb, nt),
        in_specs=in_specs,
        out_specs=[pl.BlockSpec((t_blk, D_GRP), lambda b, ti: (b * nt + ti, 0)), state],
        out_shape=[jax.ShapeDtypeStruct((nb * t, D_GRP), f32),
                   jax.ShapeDtypeStruct((nb, N_HEADS, HEAD_DIM, HEAD_DIM), f32)],
        scratch_shapes=[pltpu.VMEM((N_HEADS, HEAD_DIM, HEAD_DIM), f32)],
        compiler_params=_params("parallel", "arbitrary"),
        name="rwkv_scan",
    )(*args)


def _out_proj_body(x_ref, osb_ref, orw_ref, scale_ref, w_ref, o_ref):
    o_sb = (osb_ref[...] * scale_ref[...]).astype(bf16)
    acc = jnp.dot(o_sb, w_ref[0:D_GRP, :], preferred_element_type=f32)
    acc = acc + jnp.dot(orw_ref[...].astype(bf16), w_ref[D_GRP:, :], preferred_element_type=f32)
    o_ref[...] = x_ref[...] + acc


def _out_proj(x, o_sb, o_rw, scale, w_bf16, tm=512):
    t, d = x.shape
    return pl.pallas_call(
        _out_proj_body,
        grid=(t // tm,),
        in_specs=[pl.BlockSpec((tm, d), lambda i: (i, 0)),
                  pl.BlockSpec((tm, D_GRP), lambda i: (i, 0)),
                  pl.BlockSpec((tm, D_GRP), lambda i: (i, 0)),
                  pl.BlockSpec((1, D_GRP), lambda i: (0, 0)),
                  pl.BlockSpec(w_bf16.shape, lambda i: (0, 0))],
        out_specs=pl.BlockSpec((tm, d), lambda i: (i, 0)),
        out_shape=jax.ShapeDtypeStruct((t, d), f32),
        compiler_params=_params("parallel"),
        name="out_proj",
    )(x, o_sb, o_rw, scale.reshape(1, -1), w_bf16)


def _rms_rows(x, g):
    return x * lax.rsqrt(jnp.mean(x * x, axis=-1, keepdims=True) + RMS_EPS) * g


def _ffn_body(x_ref, g_ref, wg_ref, wu_ref, wd_ref, o_ref, h_scr):
    f = pl.program_id(1)

    @pl.when(f == 0)
    def _():
        x = x_ref[...]
        h_scr[...] = _rms_rows(x, g_ref[...]).astype(bf16)
        o_ref[...] = x

    h = h_scr[...]
    act = jax.nn.silu(jnp.dot(h, wg_ref[...], preferred_element_type=f32)) * jnp.dot(h, wu_ref[...], preferred_element_type=f32)
    o_ref[...] += jnp.dot(act.astype(bf16), wd_ref[...], preferred_element_type=f32)


def _ffn(x, g, wg, wu, wd, tm=512, tf=512):
    t, d = x.shape
    dff = wg.shape[1]
    return pl.pallas_call(
        _ffn_body,
        grid=(t // tm, dff // tf),
        in_specs=[pl.BlockSpec((tm, d), lambda i, f: (i, 0)),
                  pl.BlockSpec((1, d), lambda i, f: (0, 0)),
                  pl.BlockSpec((d, tf), lambda i, f: (0, f)),
                  pl.BlockSpec((d, tf), lambda i, f: (0, f)),
                  pl.BlockSpec((tf, d), lambda i, f: (f, 0))],
        out_specs=pl.BlockSpec((tm, d), lambda i, f: (i, 0)),
        out_shape=jax.ShapeDtypeStruct((t, d), f32),
        scratch_shapes=[pltpu.VMEM((tm, d), bf16)],
        compiler_params=_params("parallel", "arbitrary"),
        name="ffn_dense",
    )(x, g.reshape(1, d), wg, wu, wd)


def _top2_gates(logits):
    lane = _iota(logits.shape, 1)
    valid = lane < N_EXPERTS
    m = jnp.max(jnp.where(valid, logits, -jnp.inf), axis=-1, keepdims=True)
    e = jnp.where(valid, jnp.exp(logits - m), 0.0)
    p = e / jnp.sum(e, axis=-1, keepdims=True)
    p1 = jnp.max(p, axis=-1, keepdims=True)
    i1 = jnp.min(jnp.where(valid & (p == p1), lane, LANES), axis=-1, keepdims=True)
    rest = jnp.where(valid & (lane != i1), p, -1.0)
    p2 = jnp.max(rest, axis=-1, keepdims=True)
    i2 = jnp.min(jnp.where(rest == p2, lane, LANES), axis=-1, keepdims=True)
    tot = p1 + p2
    return jnp.where(lane == i1, p1 / tot, jnp.where(lane == i2, p2 / tot, 0.0))


def _moe_body(x_ref, g_ref, router_ref, wg_ref, wu_ref, wd_ref, o_ref, h_scr, gates_scr):
    e = pl.program_id(1)
    f = pl.program_id(2)

    @pl.when((e == 0) & (f == 0))
    def _():
        x = x_ref[...]
        hn = _rms_rows(x, g_ref[...])
        h_scr[...] = hn.astype(bf16)
        gates_scr[...] = _top2_gates(_mm(hn, router_ref[...], NN, 3, 3))
        o_ref[...] = x

    h = h_scr[...]
    gates = gates_scr[...]
    gate = jnp.sum(jnp.where(_iota(gates.shape, 1) == e, gates, 0.0), axis=-1, keepdims=True)
    act = jax.nn.silu(jnp.dot(h, wg_ref[0], preferred_element_type=f32)) * jnp.dot(h, wu_ref[0], preferred_element_type=f32)
    o_ref[...] += jnp.dot((act * gate).astype(bf16), wd_ref[0], preferred_element_type=f32)


def _moe(x, g, router, wg, wu, wd, tm=512, tf=256):
    t, d = x.shape
    ne, _, de = wg.shape
    router_pad = jnp.zeros((d, LANES), f32).at[:, :ne].set(router)
    return pl.pallas_call(
        _moe_body,
        grid=(t // tm, ne, de // tf),
        in_specs=[pl.BlockSpec((tm, d), lambda i, e, f: (i, 0)),
                  pl.BlockSpec((1, d), lambda i, e, f: (0, 0)),
                  pl.BlockSpec((d, LANES), lambda i, e, f: (0, 0)),
                  pl.BlockSpec((1, d, tf), lambda i, e, f: (e, 0, f)),
                  pl.BlockSpec((1, d, tf), lambda i, e, f: (e, 0, f)),
                  pl.BlockSpec((1, tf, d), lambda i, e, f: (e, f, 0))],
        out_specs=pl.BlockSpec((tm, d), lambda i, e, f: (i, 0)),
        out_shape=jax.ShapeDtypeStruct((t, d), f32),
        scratch_shapes=[pltpu.VMEM((tm, d), bf16), pltpu.VMEM((tm, LANES), f32)],
        compiler_params=_params("parallel", "arbitrary", "arbitrary"),
        name="moe",
    )(x, g.reshape(1, d), router_pad, wg, wu, wd)


def kernel(x_prompt, x_sample, cache_k, cache_v, page_table, state_wkv, state_shift, norm_mix_g, w_in, sb_q_norm_g, sb_k_norm_g, sb_logit_bias, sb_out_scale, rw_shift_mu, rw_w0, rw_w2, rw_a0, rw_a2, rw_g2, rw_k_k, rw_k_a, rw_r_k, rw_ln_g, rw_ln_b, w_out, norm_ffn_g, ffn_w_gate, ffn_w_up, ffn_w_down, moe_router, moe_w_gate, moe_w_up, moe_w_down):
    bp, tp, d = x_prompt.shape
    nb, ts, _ = x_sample.shape
    depth = w_in.shape[0]
    n_p = bp * tp
    x = jnp.concatenate([x_prompt.reshape(n_p, d), x_sample.reshape(nb * ts, d)], axis=0)
    cache_k_t = cache_k.transpose(0, 1, 3, 4, 2)
    cache_v_t = cache_v.transpose(0, 1, 3, 4, 2)
    rw0 = 3 * D_GRP

    kp, vp, ksm, vsm, wp, wsm, sp, ssm = [], [], [], [], [], [], [], []
    for l in range(depth):
        proj = _norm_proj(x, norm_mix_g[l], w_in[l].astype(bf16))
        p_rw_p = proj[:n_p, rw0:].reshape(bp, tp, N_SHIFT)
        p_rw_s = proj[n_p:, rw0:].reshape(nb, ts, N_SHIFT)
        prev = jnp.concatenate([
            jnp.concatenate([jnp.zeros((bp, 1, N_SHIFT), f32), p_rw_p[:, :-1]], axis=1).reshape(n_p, N_SHIFT),
            jnp.concatenate([state_shift[l][:, None, :], p_rw_s[:, :-1]], axis=1).reshape(nb * ts, N_SHIFT)], axis=0)
        q, k, r, lw, k2, v_rw, a, b, gate = _mix_prep(
            proj, prev, rw_shift_mu[l], sb_q_norm_g[l], sb_k_norm_g[l], rw_w0[l], rw_w2[l], rw_a0[l], rw_a2[l],
            rw_g2[l], rw_k_k[l], rw_k_a[l])
        v = proj[:, 2 * D_GRP:3 * D_GRP]

        o_sb_p = _sb_prompt(q, k, proj, sb_logit_bias[l], bp, tp)
        o_sb_s = _sb_sample(q, k, proj, sb_logit_bias[l], cache_k_t, cache_v_t, page_table, l, n_p, ts)
        o_sb = jnp.concatenate([o_sb_p, o_sb_s.transpose(0, 2, 1, 3).reshape(nb * ts, D_GRP)], axis=0)

        rw_in = (r, lw, k2, v_rw, a, b, gate)
        o_rw_p, wkv_p = _rwkv_scan(rw_in, rw_r_k[l], rw_ln_g[l], rw_ln_b[l], None, bp, tp, 0, c=64, t_blk=256)
        o_rw_s, wkv_s = _rwkv_scan(rw_in, rw_r_k[l], rw_ln_g[l], rw_ln_b[l], state_wkv[l], nb, ts, n_p, c=ts, t_blk=ts)
        o_rw = jnp.concatenate([o_rw_p, o_rw_s], axis=0)

        x = _out_proj(x, o_sb, o_rw, sb_out_scale[l], w_out[l].astype(bf16))
        i = l // 2
        if l % 2 == 0:
            x = _ffn(x, norm_ffn_g[l], ffn_w_gate[i].astype(bf16), ffn_w_up[i].astype(bf16), ffn_w_down[i].astype(bf16))
        else:
            x = _moe(x, norm_ffn_g[l], moe_router[i], moe_w_gate[i].astype(bf16), moe_w_up[i].astype(bf16),
                     moe_w_down[i].astype(bf16))

        kp.append(k[:n_p].reshape(bp, tp, N_HEADS, HEAD_DIM))
        vp.append(v[:n_p].reshape(bp, tp, N_HEADS, HEAD_DIM))
        ksm.append(k[n_p:].reshape(nb, ts, N_HEADS, HEAD_DIM))
        vsm.append(v[n_p:].reshape(nb, ts, N_HEADS, HEAD_DIM))
        wp.append(wkv_p)
        wsm.append(wkv_s)
        sp.append(p_rw_p[:, -1])
        ssm.append(p_rw_s[:, -1])

    return (x[:n_p].reshape(bp, tp, d), x[n_p:].reshape(nb, ts, d), jnp.stack(kp), jnp.stack(vp), jnp.stack(ksm),
            jnp.stack(vsm), jnp.stack(wp), jnp.stack(wsm), jnp.stack(sp), jnp.stack(ssm))
```

```python
import functools

import jax
import jax.numpy as jnp
from jax import lax
from jax.experimental import pallas as pl
from jax.experimental.pallas import tpu as pltpu

f32 = jnp.float32
bf16 = jnp.bfloat16

D_MODEL = 2048
HEAD_DIM = 64
N_HEADS = 16
D_GRP = N_HEADS * HEAD_DIM
R_DECAY = 64
R_AAA = 64
R_GATE = 128
N_SHIFT = 3 * D_GRP + R_DECAY + R_AAA + R_GATE
N_IN = 3 * D_GRP + N_SHIFT
N_EXPERTS = 8
PAGE = 128
RMS_EPS = 1e-6
LN_X_EPS = 64e-5
QK_SCALE = HEAD_DIM ** -0.5

LANES = 128
SUBLANES = 8
VMEM_LIMIT = 56 * 1024 * 1024

NN = (((1,), (0,)), ((), ()))
NT = (((1,), (1,)), ((), ()))
TN = (((0,), (0,)), ((), ()))


def _params(*sem):
    return pltpu.CompilerParams(dimension_semantics=sem, vmem_limit_bytes=VMEM_LIMIT)


def _split(x, n):
    out, r = [], x
    for i in range(n):
        h = r.astype(bf16)
        out.append(h)
        if i + 1 < n:
            r = r - h.astype(f32)
    return out


def _mm(a, b, dn=NN, pa=1, pb=1):
    sa = [a] if a.dtype == bf16 else _split(a, pa)
    sb = [b] if b.dtype == bf16 else _split(b, pb)
    order = max(len(sa), len(sb))
    acc = None
    for i, ai in enumerate(sa):
        for j, bj in enumerate(sb):
            if i + j < order:
                t = lax.dot_general(ai, bj, dn, preferred_element_type=f32)
                acc = t if acc is None else acc + t
    return acc


def _softplus(z):
    return jnp.maximum(z, 0.0) + jnp.log1p(jnp.exp(-jnp.abs(z)))


def _iota(shape, dim):
    return lax.broadcasted_iota(jnp.int32, shape, dim)


def _head(x, h):
    return x[:, h * HEAD_DIM:(h + 1) * HEAD_DIM]


def _norm_proj_body(x_ref, g_ref, w_ref, o_ref, h_scr):
    @pl.when(pl.program_id(1) == 0)
    def _():
        x = x_ref[...]
        y = x * lax.rsqrt(jnp.mean(x * x, axis=-1, keepdims=True) + RMS_EPS)
        h_scr[...] = (y * g_ref[...]).astype(bf16)

    o_ref[...] = jnp.dot(h_scr[...], w_ref[...], preferred_element_type=f32)


def _norm_proj(x, g, w_bf16, tm=512, tn=1280):
    t, d = x.shape
    n = w_bf16.shape[1]
    return pl.pallas_call(
        _norm_proj_body,
        grid=(t // tm, n // tn),
        in_specs=[pl.BlockSpec((tm, d), lambda i, j: (i, 0)),
                  pl.BlockSpec((1, d), lambda i, j: (0, 0)),
                  pl.BlockSpec((d, tn), lambda i, j: (0, j))],
        out_specs=pl.BlockSpec((tm, tn), lambda i, j: (i, j)),
        out_shape=jax.ShapeDtypeStruct((t, n), f32),
        scratch_shapes=[pltpu.VMEM((tm, d), bf16)],
        compiler_params=_params("parallel", "arbitrary"),
        name="norm_proj",
    )(x, g.reshape(1, d), w_bf16)


def _mix_prep_body(proj_ref, before_ref, state_ref, mu_ref, gq_ref, gk_ref, w0_ref, w2_ref, a0_ref, a2_ref, g2_ref,
                   kk_ref, ka_ref, r1_ref, r2_ref,
                   q_ref, k_ref, r_ref, lw_ref, k2_ref, v_ref, a_ref, b_ref, g_ref,
                   *, prompt_tiles, prompt_len, sample_len):
    tm = proj_ref.shape[0]
    i = pl.program_id(0)
    seg_ones = r1_ref[...]
    seg_bcast = r2_ref[...]
    row = _iota((tm, 1), 0)
    is_sample = i >= prompt_tiles
    keep_first = jnp.where((i * tm) % prompt_len == 0, 0.0, 1.0)
    n_seq = tm // sample_len
    pick = ((_iota((tm, n_seq), 0) == _iota((tm, n_seq), 1) * sample_len)).astype(bf16)
    opens_sample = row % sample_len == 0

    def head_sum(x):
        return _mm(x, seg_ones, NN, 3, 1)

    def head_spread(s):
        return _mm(s, seg_bcast, NN, 3, 1)

    q = proj_ref[:, 0:D_GRP]
    qn = q * head_spread(lax.rsqrt(head_sum(q * q) * (1.0 / HEAD_DIM) + RMS_EPS))
    q_ref[...] = qn * gq_ref[...] * QK_SCALE
    k = proj_ref[:, D_GRP:2 * D_GRP]
    kn = k * head_spread(lax.rsqrt(head_sum(k * k) * (1.0 / HEAD_DIM) + RMS_EPS))
    k_ref[...] = kn * gk_ref[...]

    base = 3 * D_GRP

    def shifted(lo, hi):
        p = proj_ref[:, base + lo:base + hi]
        before = jnp.where(row == 0, before_ref[SUBLANES - 1:SUBLANES, base + lo:base + hi], pltpu.roll(p, 1, axis=0))
        in_prompt = jnp.where(row == 0, before * keep_first, before)
        in_sample = jnp.where(opens_sample, _mm(pick, state_ref[:, lo:hi], NN, 1, 3), before)
        prev = jnp.where(is_sample, in_sample, in_prompt)
        return p + (prev - p) * mu_ref[:, lo:hi]

    r = shifted(0, D_GRP)
    kr = shifted(D_GRP, 2 * D_GRP)
    v = shifted(2 * D_GRP, 3 * D_GRP)
    tail = shifted(3 * D_GRP, N_SHIFT)
    wd = tail[:, 0:R_DECAY]
    ad = tail[:, R_DECAY:R_DECAY + R_AAA]
    gd = tail[:, R_DECAY + R_AAA:]

    w_pre = w0_ref[...] + _mm(jnp.tanh(wd), w2_ref[...])
    w_log = -_softplus(-w_pre) - 0.5
    lw_ref[...] = -jnp.exp(w_log)
    a_lr = jax.nn.sigmoid(a0_ref[...] + _mm(ad, a2_ref[...]))
    g_ref[...] = _mm(jax.nn.sigmoid(gd), g2_ref[...])

    kk = kr * kk_ref[...]
    kk = kk * head_spread(lax.rsqrt(jnp.maximum(head_sum(kk * kk), 1e-24)))
    r_ref[...] = r
    k2_ref[...] = kr * (1.0 + (a_lr - 1.0) * ka_ref[...])
    v_ref[...] = v
    a_ref[...] = -kk
    b_ref[...] = kk * a_lr


def _mix_prep(proj, shift_state, n_prompt, prompt_len, sample_len, mu, gq, gk, w0, w2, a0, a2, g2, k_k, k_a, tm=128):
    t = proj.shape[0]
    heads = _iota((D_GRP, LANES), 0) // HEAD_DIM
    seg_ones = (heads == _iota((D_GRP, LANES), 1)).astype(bf16)
    seg_bcast = seg_ones.T
    prompt_tiles = n_prompt // tm
    n_seq = tm // sample_len
    last_state_blk = shift_state.shape[0] // n_seq - 1
    row = lambda n: pl.BlockSpec((1, n), lambda i: (0, 0))
    full = lambda a: pl.BlockSpec(a.shape, lambda i: (0, 0))
    tok = lambda n: pl.BlockSpec((tm, n), lambda i: (i, 0))
    before = pl.BlockSpec((SUBLANES, N_IN), lambda i: (jnp.maximum(i * (tm // SUBLANES) - 1, 0), 0))
    state = pl.BlockSpec((n_seq, N_SHIFT), lambda i: (jnp.clip(i - prompt_tiles, 0, last_state_blk), 0))
    outs = pl.pallas_call(
        functools.partial(_mix_prep_body, prompt_tiles=prompt_tiles, prompt_len=prompt_len, sample_len=sample_len),
        grid=(t // tm,),
        in_specs=[tok(N_IN), before, state, row(N_SHIFT), row(D_GRP), row(D_GRP), row(D_GRP), full(w2), row(D_GRP),
                  full(a2), full(g2), row(D_GRP), row(D_GRP), full(seg_ones), full(seg_bcast)],
        out_specs=[tok(D_GRP)] * 9,
        out_shape=[jax.ShapeDtypeStruct((t, D_GRP), f32)] * 9,
        compiler_params=_params("parallel"),
        name="mix_prep",
    )(proj, proj, shift_state, mu.reshape(1, -1), jnp.tile(gq, N_HEADS).reshape(1, -1), jnp.tile(gk, N_HEADS).reshape(1, -1),
      w0.reshape(1, -1), w2.astype(bf16), a0.reshape(1, -1), a2.astype(bf16), g2.astype(bf16),
      k_k.reshape(1, -1), k_a.reshape(1, -1), seg_ones, seg_bcast)
    return outs


def _suffix_matrix(tk):
    r = _iota((2 * tk, 2 * tk), 0) % tk
    c = _iota((2 * tk, 2 * tk), 1)
    return jnp.where((c >= tk) | (r > c), 1.0, 0.0).astype(bf16)


def _sb_scores(z, suffix_mat, mask):
    tk = z.shape[1]
    sp = jnp.maximum(z, 0.0) + jnp.log(1.0 + jnp.exp(-jnp.abs(z)))
    log_1m = -sp
    if mask is not None:
        log_1m = jnp.where(mask, log_1m, 0.0)
    hi = log_1m.astype(bf16)
    lo = (log_1m - hi.astype(f32)).astype(bf16)
    st = jnp.dot(jnp.concatenate([hi, lo], axis=1), suffix_mat, preferred_element_type=f32)
    return (z - sp) + st[:, :tk], st[:, tk:]


def _sb_prompt_body(q_ref, k_ref, v_ref, bias_ref, sfx_ref, o_ref, carry_scr, acc_scr, *, tq, tk, hp):
    i = pl.program_id(2)
    ratio = tq // tk
    hs = range(hp)
    lane_head = _iota((tq, hp * HEAD_DIM), 1) // HEAD_DIM
    qall = q_ref[...]
    qh = [jnp.where(lane_head == h, qall, 0.0).astype(bf16) for h in hs]
    bias = [bias_ref[0, h:h + 1, :] for h in hs]
    sfx = sfx_ref[...]
    carry_scr[...] = jnp.zeros_like(carry_scr)
    acc_scr[...] = jnp.zeros_like(acc_scr)
    row = _iota((tq, tk), 0)
    col = _iota((tq, tk), 1)

    def block(j, masked):
        ks = pl.ds(pl.multiple_of(j * tk, tk), tk)
        kb = k_ref[ks, :].astype(bf16)
        vb = v_ref[ks, :].astype(bf16)
        mask = (j * tk + col < i * tq + row) if masked else None
        zs = [_mm(qh[h], kb, NT) + bias[h] for h in hs]
        sc = [_sb_scores(z, sfx, mask) for z in zs]
        ws = []
        for h in hs:
            carry = carry_scr[h]
            a = jnp.exp(sc[h][0] + carry)
            if masked:
                a = jnp.where(mask, a, 0.0)
            carry_scr[h] = carry + sc[h][1]
            ws.append(a.astype(bf16))
        os_ = [_mm(w, vb) for w in ws]
        upd = os_[hp - 1]
        for h in reversed(range(hp - 1)):
            upd = jnp.where(lane_head == h, os_[h], upd)
        acc_scr[...] += upd

    for u in range(ratio):
        block((i + 1) * ratio - 1 - u, True)

    def step(jj, c):
        block(i * ratio - 1 - jj, False)
        return c

    lax.fori_loop(0, i * ratio, step, 0)
    o_ref[...] = acc_scr[...]


def _sb_prompt(q, k, proj, bias, nb, t, tq=256, tk=PAGE, hp=4):
    groups = N_HEADS // hp
    width = hp * HEAD_DIM
    nq = t // tq
    bias_rows = jnp.broadcast_to(bias.astype(f32).reshape(groups, hp, 1), (groups, hp, tk))
    v_blk0 = 2 * D_GRP // width
    return pl.pallas_call(
        functools.partial(_sb_prompt_body, tq=tq, tk=tk, hp=hp),
        grid=(nb, groups, nq),
        in_specs=[pl.BlockSpec((tq, width), lambda b, p, i: (b * nq + i, p)),
                  pl.BlockSpec((t, width), lambda b, p, i: (b, p)),
                  pl.BlockSpec((t, width), lambda b, p, i: (b, v_blk0 + p)),
                  pl.BlockSpec((1, hp, tk), lambda b, p, i: (p, 0, 0)),
                  pl.BlockSpec((2 * tk, 2 * tk), lambda b, p, i: (0, 0))],
        out_specs=pl.BlockSpec((tq, width), lambda b, p, i: (b * nq + i, p)),
        out_shape=jax.ShapeDtypeStruct((nb * t, D_GRP), f32),
        scratch_shapes=[pltpu.VMEM((hp, tq, tk), f32), pltpu.VMEM((tq, width), f32)],
        compiler_params=_params("parallel", "parallel", "arbitrary"),
        name="sb_prompt",
    )(q, k, proj, bias_rows, _suffix_matrix(tk))


def _sb_sample_body(pt_ref, q_ref, kn_ref, vn_ref, bias_ref, sfx_ref, *refs, n_new, pp):
    kc_refs, vc_refs = refs[:pp], refs[pp:2 * pp]
    o_ref, q_scr, acc_scr, carry_scr = refs[2 * pp:]
    s = pl.program_id(1)
    rows = N_HEADS * n_new
    heads = range(N_HEADS)
    sfx = sfx_ref[...]
    bias = bias_ref[...]
    hrows = lambda x, h: x[h * n_new:(h + 1) * n_new]

    @pl.when(s == 0)
    def _():
        q = q_ref[...]
        for h in heads:
            q_scr[h] = _head(q, h)
        pad = jnp.zeros((PAGE - n_new, HEAD_DIM), f32)
        kn, vn = kn_ref[...], vn_ref[...]
        zs = [_mm(_head(q, h).astype(bf16), jnp.concatenate([_head(kn, h), pad], axis=0).astype(bf16), NT)
              for h in heads]
        causal = _iota((rows, PAGE), 1) < _iota((rows, PAGE), 0) % n_new
        lbs, tot = _sb_scores(jnp.concatenate(zs, axis=0) + bias, sfx, causal)
        a = jnp.where(causal, jnp.exp(lbs), 0.0)
        os_ = [_mm(hrows(a, h).astype(bf16), jnp.concatenate([_head(vn, h), pad], axis=0).astype(bf16))
               for h in heads]
        acc_scr[...] = jnp.concatenate(os_, axis=0)
        carry_scr[...] = tot

    qb = [q_scr[h].astype(bf16) for h in heads]
    zs = [jnp.concatenate([_mm(qb[h], kc_refs[u][0, 0, h].astype(bf16)) for h in heads], axis=0) + bias
          for u in range(pp)]
    sc = [_sb_scores(z, sfx, None) for z in zs]
    carry = carry_scr[...]
    ws = []
    for u in range(pp):
        ws.append(jnp.exp(sc[u][0] + carry))
        carry = carry + sc[u][1]
    carry_scr[...] = carry
    os_ = [[_mm(hrows(ws[u], h).astype(bf16), vc_refs[u][0, 0, h].astype(bf16), NT) for h in heads]
           for u in range(pp)]
    upd = [functools.reduce(lambda x, y: x + y, [os_[u][h] for u in range(pp)]) for h in heads]
    acc_scr[...] += jnp.concatenate(upd, axis=0)

    @pl.when(s == pl.num_programs(1) - 1)
    def _():
        acc = acc_scr[...]
        for h in heads:
            o_ref[0, h] = hrows(acc, h)


def _sb_sample(q, k, proj, bias, cache_k_t, cache_v_t, page_table, layer, row0, n_new, pp=8):
    nb, n_pages = page_table.shape
    rows = N_HEADS * n_new
    bias_rows = jnp.repeat(bias.astype(f32), n_new).reshape(rows, 1)
    blk0 = row0 // n_new
    v_blk0 = 2 * D_GRP // D_GRP

    def page(u):
        return lambda b, s, pt: (layer, pt[b, n_pages - 1 - (s * pp + u)], 0, 0, 0)

    tok = lambda cb: pl.BlockSpec((n_new, D_GRP), lambda b, s, pt: (blk0 + b, cb))
    page_specs = [pl.BlockSpec((1, 1, N_HEADS, HEAD_DIM, PAGE), page(u)) for u in range(pp)]
    grid_spec = pltpu.PrefetchScalarGridSpec(
        num_scalar_prefetch=1,
        grid=(nb, n_pages // pp),
        in_specs=[tok(0), tok(0), tok(v_blk0),
                  pl.BlockSpec((rows, 1), lambda b, s, pt: (0, 0)),
                  pl.BlockSpec((2 * PAGE, 2 * PAGE), lambda b, s, pt: (0, 0))] + page_specs + page_specs,
        out_specs=pl.BlockSpec((1, N_HEADS, n_new, HEAD_DIM), lambda b, s, pt: (b, 0, 0, 0)),
        scratch_shapes=[pltpu.VMEM((N_HEADS, n_new, HEAD_DIM), f32), pltpu.VMEM((rows, HEAD_DIM), f32),
                        pltpu.VMEM((rows, PAGE), f32)],
    )
    return pl.pallas_call(
        functools.partial(_sb_sample_body, n_new=n_new, pp=pp),
        grid_spec=grid_spec,
        out_shape=jax.ShapeDtypeStruct((nb, N_HEADS, n_new, HEAD_DIM), f32),
        compiler_params=_params("parallel", "arbitrary"),
        name="sb_sample",
    )(page_table, q, k, proj, bias_rows, _suffix_matrix(PAGE),
      *([cache_k_t] * pp), *([cache_v_t] * pp))


def _unit_lower_inverses(a_strict, c):
    r = _iota((c, c), 0)
    col = _iota((c, c), 1)
    mm = lambda x, y: _mm(x, y, NN, 2, 2)
    base = SUBLANES
    ps = [jnp.where(r // base == col // base, a, 0.0) for a in a_strict]
    eye = jnp.where(r == col, 1.0, 0.0)
    ts = [eye + p for p in ps]
    for _ in range(2):
        ps = [mm(p, p) for p in ps]
        ts = [t + mm(t, p) for t, p in zip(ts, ps)]
    s = base
    while s < c:
        join = (r // (2 * s) == col // (2 * s)) & (r // s != col // s)
        low = [mm(jnp.where(join, a, 0.0), t) for a, t in zip(a_strict, ts)]
        ts = [t + mm(t, x) for t, x in zip(ts, low)]
        s *= 2
    return ts


def _rwkv_body(*refs, c, n_chunks, has_init):
    if has_init:
        (r_ref, lw_ref, k_ref, v_ref, a_ref, b_ref, gate_ref, rk_ref, lng_ref, lnb_ref, s0_ref,
         o_ref, sout_ref, s_scr) = refs
    else:
        (r_ref, lw_ref, k_ref, v_ref, a_ref, b_ref, gate_ref, rk_ref, lng_ref, lnb_ref,
         o_ref, sout_ref, s_scr) = refs
    tstep = pl.program_id(1)
    heads = range(N_HEADS)

    @pl.when(tstep == 0)
    def _():
        s_scr[...] = s0_ref[0] if has_init else jnp.zeros_like(s_scr)

    row = _iota((c, c), 0)
    col = _iota((c, c), 1)
    incl = jnp.where(row >= col, 1.0, 0.0).astype(bf16)
    row2 = _iota((2 * c, 2 * c), 0)
    col2 = _iota((2 * c, 2 * c), 1)
    lower = (col2 % c) < jnp.where(row2 < c, row2, row2 - c + 1)
    mm = lambda x, y, dn=NN: _mm(x, y, dn, 2, 2)
    stack = lambda x, y: jnp.concatenate([x, y], axis=0)

    def chunk(ci, carry):
        ts = pl.ds(pl.multiple_of(ci * c, c), c)
        r, lw, k, v = r_ref[ts, :], lw_ref[ts, :], k_ref[ts, :], v_ref[ts, :]
        cum = _mm(incl, lw, NN, 1, 3)
        grow = jnp.exp(-cum)
        a_t = a_ref[ts, :] * jnp.exp(cum - lw)
        r_t = r * jnp.exp(cum)
        b_t = b_ref[ts, :] * grow
        k_t = k * grow
        end = jnp.exp(cum[c - 1:c, :])
        left = [stack(_head(a_t, h), _head(r_t, h)) for h in heads]
        right = [stack(_head(b_t, h), _head(k_t, h)) for h in heads]
        vs = [_head(v, h) for h in heads]
        s0 = [s_scr[h] for h in heads]
        gram = [jnp.where(lower, mm(l, rt, NT), 0.0) for l, rt in zip(left, right)]
        from_state = [mm(l, s, NT) for l, s in zip(left, s0)]
        t_inv = _unit_lower_inverses([g[:c, :c] for g in gram], c)
        akv = [mm(g[:c, c:], x) for g, x in zip(gram, vs)]
        us = [mm(t, f[:c] + x) for t, f, x in zip(t_inv, from_state, akv)]
        uv = [stack(u, x) for u, x in zip(us, vs)]
        ys = [f[c:] + mm(g[c:], x) for f, g, x in zip(from_state, gram, uv)]
        s_new = [(s + mm(x, rt, TN)) * _head(end, h) for h, (s, x, rt) in enumerate(zip(s0, uv, right))]
        for h in heads:
            s_scr[h] = s_new[h]
            y = ys[h]
            mu = jnp.mean(y, axis=-1, keepdims=True)
            yc = y - mu
            var = jnp.mean(yc * yc, axis=-1, keepdims=True)
            yn = yc * lax.rsqrt(var + LN_X_EPS) * _head(lng_ref, h) + _head(lnb_ref, h)
            bonus = jnp.sum(_head(r, h) * _head(k, h) * _head(rk_ref, h), axis=-1, keepdims=True) * vs[h]
            o_ref[ts, h * HEAD_DIM:(h + 1) * HEAD_DIM] = (yn + bonus) * gate_ref[ts, h * HEAD_DIM:(h + 1) * HEAD_DIM]
        return carry

    lax.fori_loop(0, n_chunks, chunk, 0)

    @pl.when(tstep == pl.num_programs(1) - 1)
    def _():
        sout_ref[0] = s_scr[...]


def _rwkv_scan(rw_in, r_k, ln_g, ln_b, s0, nb, t, row0, c, t_blk):
    nt = t // t_blk
    blk0 = row0 // t_blk
    seq = pl.BlockSpec((t_blk, D_GRP), lambda b, ti: (blk0 + b * nt + ti, 0))
    per_lane = pl.BlockSpec((1, D_GRP), lambda b, ti: (0, 0))
    state = pl.BlockSpec((1, N_HEADS, HEAD_DIM, HEAD_DIM), lambda b, ti: (b, 0, 0, 0))
    has_init = s0 is not None
    args = list(rw_in) + [r_k.reshape(1, D_GRP), ln_g.reshape(1, D_GRP), ln_b.reshape(1, D_GRP)]
    in_specs = [seq] * 7 + [per_lane] * 3
    if has_init:
        args.append(s0)
        in_specs.append(state)
    return pl.pallas_call(
        functools.partial(_rwkv_body, c=c, n_chunks=t_blk // c, has_init=has_init),
        grid=(nb, nt),
        in_specs=in_specs,
        out_specs=[pl.BlockSpec((t_blk, D_GRP), lambda b, ti: (b * nt + ti, 0)), state],
        out_shape=[jax.ShapeDtypeStruct((nb * t, D_GRP), f32),
                   jax.ShapeDtypeStruct((nb, N_HEADS, HEAD_DIM, HEAD_DIM), f32)],
        scratch_shapes=[pltpu.VMEM((N_HEADS, HEAD_DIM, HEAD_DIM), f32)],
        compiler_params=_params("parallel", "arbitrary"),
        name="rwkv_scan",
    )(*args)


def _out_proj_body(x_ref, osb_ref, orw_ref, scale_ref, w_ref, o_ref):
    o_sb = (osb_ref[...] * scale_ref[...]).astype(bf16)
    acc = jnp.dot(o_sb, w_ref[0:D_GRP, :], preferred_element_type=f32)
    acc = acc + jnp.dot(orw_ref[...].astype(bf16), w_ref[D_GRP:, :], preferred_element_type=f32)
    o_ref[...] = x_ref[...] + acc


def _out_proj(x, o_sb, o_rw, scale, w_bf16, tm=512):
    t, d = x.shape
    return pl.pallas_call(
        _out_proj_body,
        grid=(t // tm,),
        in_specs=[pl.BlockSpec((tm, d), lambda i: (i, 0)),
                  pl.BlockSpec((tm, D_GRP), lambda i: (i, 0)),
                  pl.BlockSpec((tm, D_GRP), lambda i: (i, 0)),
                  pl.BlockSpec((1, D_GRP), lambda i: (0, 0)),
                  pl.BlockSpec(w_bf16.shape, lambda i: (0, 0))],
        out_specs=pl.BlockSpec((tm, d), lambda i: (i, 0)),
        out_shape=jax.ShapeDtypeStruct((t, d), f32),
        compiler_params=_params("parallel"),
        name="out_proj",
    )(x, o_sb, o_rw, scale.reshape(1, -1), w_bf16)


def _rms_rows(x, g):
    return x * lax.rsqrt(jnp.mean(x * x, axis=-1, keepdims=True) + RMS_EPS) * g


def _ffn_body(x_ref, g_ref, wg_ref, wu_ref, wd_ref, o_ref, h_scr):
    f = pl.program_id(1)

    @pl.when(f == 0)
    def _():
        x = x_ref[...]
        h_scr[...] = _rms_rows(x, g_ref[...]).astype(bf16)
        o_ref[...] = x

    h = h_scr[...]
    act = jax.nn.silu(jnp.dot(h, wg_ref[...], preferred_element_type=f32)) * jnp.dot(h, wu_ref[...], preferred_element_type=f32)
    o_ref[...] += jnp.dot(act.astype(bf16), wd_ref[...], preferred_element_type=f32)


def _ffn(x, g, wg, wu, wd, tm=512, tf=512):
    t, d = x.shape
    dff = wg.shape[1]
    return pl.pallas_call(
        _ffn_body,
        grid=(t // tm, dff // tf),
        in_specs=[pl.BlockSpec((tm, d), lambda i, f: (i, 0)),
                  pl.BlockSpec((1, d), lambda i, f: (0, 0)),
                  pl.BlockSpec((d, tf), lambda i, f: (0, f)),
                  pl.BlockSpec((d, tf), lambda i, f: (0, f)),
                  pl.BlockSpec((tf, d), lambda i, f: (f, 0))],
        out_specs=pl.BlockSpec((tm, d), lambda i, f: (i, 0)),
        out_shape=jax.ShapeDtypeStruct((t, d), f32),
        scratch_shapes=[pltpu.VMEM((tm, d), bf16)],
        compiler_params=_params("parallel", "arbitrary"),
        name="ffn_dense",
    )(x, g.reshape(1, d), wg, wu, wd)


def _top2_gates(logits):
    lane = _iota(logits.shape, 1)
    valid = lane < N_EXPERTS
    m = jnp.max(jnp.where(valid, logits, -jnp.inf), axis=-1, keepdims=True)
    e = jnp.where(valid, jnp.exp(logits - m), 0.0)
    p = e / jnp.sum(e, axis=-1, keepdims=True)
    p1 = jnp.max(p, axis=-1, keepdims=True)
    i1 = jnp.min(jnp.where(valid & (p == p1), lane, LANES), axis=-1, keepdims=True)
    rest = jnp.where(valid & (lane != i1), p, -1.0)
    p2 = jnp.max(rest, axis=-1, keepdims=True)
    i2 = jnp.min(jnp.where(rest == p2, lane, LANES), axis=-1, keepdims=True)
    tot = p1 + p2
    gates = jnp.where(lane == i1, p1 / tot, jnp.where(lane == i2, p2 / tot, 0.0))
    chosen = (lane == i1 + N_EXPERTS) | (lane == i2 + N_EXPERTS)
    return jnp.where(chosen, 1.0, gates)


def _router_body(x_ref, g_ref, router_ref, h_ref, gates_ref):
    hn = _rms_rows(x_ref[...], g_ref[...])
    h_ref[...] = hn
    gates_ref[...] = _top2_gates(_mm(hn, router_ref[...], NN, 3, 3))


def _router(x, g, router, tm=512):
    t, d = x.shape
    router_pad = jnp.zeros((d, LANES), f32).at[:, :router.shape[1]].set(router)
    return pl.pallas_call(
        _router_body,
        grid=(t // tm,),
        in_specs=[pl.BlockSpec((tm, d), lambda i: (i, 0)),
                  pl.BlockSpec((1, d), lambda i: (0, 0)),
                  pl.BlockSpec((d, LANES), lambda i: (0, 0))],
        out_specs=[pl.BlockSpec((tm, d), lambda i: (i, 0)), pl.BlockSpec((tm, LANES), lambda i: (i, 0))],
        out_shape=[jax.ShapeDtypeStruct((t, d), f32), jax.ShapeDtypeStruct((t, LANES), f32)],
        compiler_params=_params("parallel"),
        name="moe_router",
    )(x, g.reshape(1, d), router_pad)


def _expert_rows_body(te_ref, nr_ref, tok_ref, dst_ref, h_hbm, wg_ref, wu_ref, wd_ref, out_hbm,
                      xrow, xb, acc, gsem, ssem, *, tm):
    i = pl.program_id(0)
    f = pl.program_id(1)
    n_real = nr_ref[i]
    used = n_real > 0

    def gather(r):
        return pltpu.make_async_copy(h_hbm.at[pl.ds(tok_ref[0, 0, r], 1), :], xrow.at[pl.ds(r, 1), :], gsem)

    def scatter(r):
        return pltpu.make_async_copy(acc.at[pl.ds(r, 1), :], out_hbm.at[pl.ds(dst_ref[0, 0, r], 1), :], ssem)

    def for_rows(n, fn):
        def octet(o, c):
            for u in range(SUBLANES):
                fn(o * SUBLANES + u)
            return c
        full = n // SUBLANES
        lax.fori_loop(0, full, octet, 0)

        def one(r, c):
            fn(r)
            return c
        lax.fori_loop(full * SUBLANES, n, one, 0)

    @pl.when(used & (f == 0))
    def _():
        for_rows(tm, lambda r: gather(r).start())
        for_rows(tm, lambda r: gather(r).wait())
        xb[...] = xrow[...].astype(bf16)
        acc[...] = jnp.zeros_like(acc)

    @pl.when(used)
    def _():
        h = xb[...]
        act = jax.nn.silu(jnp.dot(h, wg_ref[0], preferred_element_type=f32)) * jnp.dot(h, wu_ref[0], preferred_element_type=f32)
        acc[...] += jnp.dot(act.astype(bf16), wd_ref[0], preferred_element_type=f32)

    @pl.when(used & (f == pl.num_programs(1) - 1))
    def _():
        for_rows(n_real, lambda r: scatter(r).start())
        for_rows(n_real, lambda r: scatter(r).wait())


def _combine_body(x_ref, y_ref, g_ref, o_ref):
    d = x_ref.shape[1]
    g = g_ref[...]
    o_ref[...] = x_ref[...] + (y_ref[:, :d] * g[:, 0:1] + y_ref[:, d:] * g[:, 1:2])


def _moe(x, g, router, wg, wu, wd, tm=512, tf=256):
    t, d = x.shape
    ne, _, de = wg.shape
    h, gates = _router(x, g, router)

    sel = (gates[:, ne:2 * ne] > 0.0).astype(jnp.int32)
    cnt = sel.sum(axis=0)
    cnt_pad = (cnt + tm - 1) // tm * tm
    ends = jnp.cumsum(cnt_pad)
    rank = jnp.cumsum(sel, axis=0) - sel
    slot = jnp.cumsum(sel, axis=1) - sel
    n_rows = 2 * t + ne * tm
    n_tiles = n_rows // tm
    pos = jnp.where(sel > 0, (ends - cnt_pad)[None, :] + rank, n_rows).reshape(-1)
    token = jnp.broadcast_to(jnp.arange(t, dtype=jnp.int32)[:, None], (t, ne))
    token_of = jnp.zeros((n_rows,), jnp.int32).at[pos].set(token.reshape(-1), mode="drop")
    dest = jnp.zeros((n_rows,), jnp.int32).at[pos].set((2 * token + slot).reshape(-1), mode="drop")
    n_used = ends[-1] // tm
    tile_row0 = jnp.arange(n_tiles, dtype=jnp.int32) * tm
    tile_expert = jnp.minimum(jnp.sum(tile_row0[:, None] >= ends[None, :], axis=1), ne - 1).astype(jnp.int32)
    group_row0 = (ends - cnt_pad)[tile_expert]
    tile_real = jnp.where(tile_row0 < ends[-1], jnp.clip(cnt[tile_expert] - (tile_row0 - group_row0), 0, tm), 0)
    tile_real = tile_real.astype(jnp.int32)
    tile_expert = tile_expert[jnp.minimum(jnp.arange(n_tiles), n_used - 1)]
    slot_gate = jnp.stack([jnp.sum(jnp.where((sel > 0) & (slot == s), gates[:, :ne], 0.0), axis=1) for s in range(2)],
                          axis=1)
    slot_gate = jnp.zeros((t, LANES), f32).at[:, :2].set(slot_gate)

    nf = de // tf
    fblk = lambda i, f, te, nr: jnp.where(nr[i] > 0, f, nf - 1)
    smem_row = pl.BlockSpec((1, 1, tm), lambda i, f, te, nr: (i, 0, 0), memory_space=pltpu.SMEM)
    grid_spec = pltpu.PrefetchScalarGridSpec(
        num_scalar_prefetch=2,
        grid=(n_tiles, nf),
        in_specs=[smem_row, smem_row,
                  pl.BlockSpec(memory_space=pl.ANY),
                  pl.BlockSpec((1, d, tf), lambda i, f, te, nr: (te[i], 0, fblk(i, f, te, nr))),
                  pl.BlockSpec((1, d, tf), lambda i, f, te, nr: (te[i], 0, fblk(i, f, te, nr))),
                  pl.BlockSpec((1, tf, d), lambda i, f, te, nr: (te[i], fblk(i, f, te, nr), 0))],
        out_specs=pl.BlockSpec(memory_space=pl.ANY),
        scratch_shapes=[pltpu.VMEM((tm, d), f32), pltpu.VMEM((tm, d), bf16), pltpu.VMEM((tm, d), f32),
                        pltpu.SemaphoreType.DMA(()), pltpu.SemaphoreType.DMA(())],
    )
    y = pl.pallas_call(
        functools.partial(_expert_rows_body, tm=tm),
        grid_spec=grid_spec,
        out_shape=jax.ShapeDtypeStruct((2 * t, d), f32),
        compiler_params=pltpu.CompilerParams(dimension_semantics=("arbitrary", "arbitrary"),
                                             vmem_limit_bytes=VMEM_LIMIT, disable_bounds_checks=True),
        name="moe_experts",
    )(tile_expert, tile_real, token_of.reshape(n_tiles, 1, tm), dest.reshape(n_tiles, 1, tm), h, wg, wu, wd)

    tc = 512
    return pl.pallas_call(
        _combine_body,
        grid=(t // tc,),
        in_specs=[pl.BlockSpec((tc, d), lambda i: (i, 0)),
                  pl.BlockSpec((tc, 2 * d), lambda i: (i, 0)),
                  pl.BlockSpec((tc, LANES), lambda i: (i, 0))],
        out_specs=pl.BlockSpec((tc, d), lambda i: (i, 0)),
        out_shape=jax.ShapeDtypeStruct((t, d), f32),
        compiler_params=_params("parallel"),
        name="moe_combine",
    )(x, y.reshape(t, 2 * d), slot_gate)


def kernel(x_prompt, x_sample, cache_k, cache_v, page_table, state_wkv, state_shift, norm_mix_g, w_in, sb_q_norm_g, sb_k_norm_g, sb_logit_bias, sb_out_scale, rw_shift_mu, rw_w0, rw_w2, rw_a0, rw_a2, rw_g2, rw_k_k, rw_k_a, rw_r_k, rw_ln_g, rw_ln_b, w_out, norm_ffn_g, ffn_w_gate, ffn_w_up, ffn_w_down, moe_router, moe_w_gate, moe_w_up, moe_w_down):
    bp, tp, d = x_prompt.shape
    nb, ts, _ = x_sample.shape
    depth = w_in.shape[0]
    n_p = bp * tp
    x = jnp.concatenate([x_prompt.reshape(n_p, d), x_sample.reshape(nb * ts, d)], axis=0)
    cache_k_t = cache_k.transpose(0, 1, 3, 4, 2)
    cache_v_t = cache_v.transpose(0, 1, 3, 4, 2)
    rw0 = 3 * D_GRP

    kp, vp, ksm, vsm, wp, wsm, sp, ssm = [], [], [], [], [], [], [], []
    for l in range(depth):
        proj = _norm_proj(x, norm_mix_g[l], w_in[l].astype(bf16))
        q, k, r, lw, k2, v_rw, a, b, gate = _mix_prep(
            proj, state_shift[l], n_p, tp, ts, rw_shift_mu[l], sb_q_norm_g[l], sb_k_norm_g[l], rw_w0[l], rw_w2[l],
            rw_a0[l], rw_a2[l], rw_g2[l], rw_k_k[l], rw_k_a[l])
        v = proj[:, 2 * D_GRP:3 * D_GRP]

        o_sb_p = _sb_prompt(q, k, proj, sb_logit_bias[l], bp, tp)
        o_sb_s = _sb_sample(q, k, proj, sb_logit_bias[l], cache_k_t, cache_v_t, page_table, l, n_p, ts)
        o_sb = jnp.concatenate([o_sb_p, o_sb_s.transpose(0, 2, 1, 3).reshape(nb * ts, D_GRP)], axis=0)

        rw_in = (r, lw, k2, v_rw, a, b, gate)
        o_rw_p, wkv_p = _rwkv_scan(rw_in, rw_r_k[l], rw_ln_g[l], rw_ln_b[l], None, bp, tp, 0, c=64, t_blk=256)
        o_rw_s, wkv_s = _rwkv_scan(rw_in, rw_r_k[l], rw_ln_g[l], rw_ln_b[l], state_wkv[l], nb, ts, n_p, c=ts, t_blk=ts)
        o_rw = jnp.concatenate([o_rw_p, o_rw_s], axis=0)

        x = _out_proj(x, o_sb, o_rw, sb_out_scale[l], w_out[l].astype(bf16))
        i = l // 2
        if l % 2 == 0:
            x = _ffn(x, norm_ffn_g[l], ffn_w_gate[i].astype(bf16), ffn_w_up[i].astype(bf16), ffn_w_down[i].astype(bf16))
        else:
            x = _moe(x, norm_ffn_g[l], moe_router[i], moe_w_gate[i].astype(bf16), moe_w_up[i].astype(bf16),
                     moe_w_down[i].astype(bf16))

        kp.append(k[:n_p].reshape(bp, tp, N_HEADS, HEAD_DIM))
        vp.append(v[:n_p].reshape(bp, tp, N_HEADS, HEAD_DIM))
        ksm.append(k[n_p:].reshape(nb, ts, N_HEADS, HEAD_DIM))
        vsm.append(v[n_p:].reshape(nb, ts, N_HEADS, HEAD_DIM))
        wp.append(wkv_p)
        wsm.append(wkv_s)
        sp.append(proj[:n_p].reshape(bp, tp, N_IN)[:, -1, rw0:])
        ssm.append(proj[n_p:].reshape(nb, ts, N_IN)[:, -1, rw0:])

    return (x[:n_p].reshape(bp, tp, d), x[n_p:].reshape(nb, ts, d), jnp.stack(kp), jnp.stack(vp), jnp.stack(ksm),
            jnp.stack(vsm), jnp.stack(wp), jnp.stack(wsm), jnp.stack(sp), jnp.stack(ssm))
```

```python
import functools

import jax
import jax.numpy as jnp
from jax import lax
from jax.experimental import pallas as pl
from jax.experimental.pallas import tpu as pltpu

f32 = jnp.float32
bf16 = jnp.bfloat16

D_MODEL = 2048
HEAD_DIM = 64
N_HEADS = 16
D_GRP = N_HEADS * HEAD_DIM
R_DECAY = 64
R_AAA = 64
R_GATE = 128
N_SHIFT = 3 * D_GRP + R_DECAY + R_AAA + R_GATE
N_IN = 3 * D_GRP + N_SHIFT
N_EXPERTS = 8
PAGE = 128
RMS_EPS = 1e-6
LN_X_EPS = 64e-5
QK_SCALE = HEAD_DIM ** -0.5

LANES = 128
SUBLANES = 8
VMEM_LIMIT = 56 * 1024 * 1024

NN = (((1,), (0,)), ((), ()))
NT = (((1,), (1,)), ((), ()))
TN = (((0,), (0,)), ((), ()))


def _params(*sem):
    return pltpu.CompilerParams(dimension_semantics=sem, vmem_limit_bytes=VMEM_LIMIT)


def _split(x, n):
    out, r = [], x
    for i in range(n):
        h = r.astype(bf16)
        out.append(h)
        if i + 1 < n:
            r = r - h.astype(f32)
    return out


def _mm(a, b, dn=NN, pa=1, pb=1):
    sa = [a] if a.dtype == bf16 else _split(a, pa)
    sb = [b] if b.dtype == bf16 else _split(b, pb)
    order = max(len(sa), len(sb))
    acc = None
    for i, ai in enumerate(sa):
        for j, bj in enumerate(sb):
            if i + j < order:
                t = lax.dot_general(ai, bj, dn, preferred_element_type=f32)
                acc = t if acc is None else acc + t
    return acc


def _softplus(z):
    return jnp.maximum(z, 0.0) + jnp.log1p(jnp.exp(-jnp.abs(z)))


def _iota(shape, dim):
    return lax.broadcasted_iota(jnp.int32, shape, dim)


def _head(x, h):
    return x[:, h * HEAD_DIM:(h + 1) * HEAD_DIM]


def _norm_proj_body(x_ref, g_ref, w_ref, o_ref, h_scr):
    @pl.when(pl.program_id(1) == 0)
    def _():
        x = x_ref[...]
        y = x * lax.rsqrt(jnp.mean(x * x, axis=-1, keepdims=True) + RMS_EPS)
        h_scr[...] = (y * g_ref[...]).astype(bf16)

    o_ref[...] = jnp.dot(h_scr[...], w_ref[...], preferred_element_type=f32)


def _norm_proj(x, g, w_bf16, tm=512, tn=1280):
    t, d = x.shape
    n = w_bf16.shape[1]
    return pl.pallas_call(
        _norm_proj_body,
        grid=(t // tm, n // tn),
        in_specs=[pl.BlockSpec((tm, d), lambda i, j: (i, 0)),
                  pl.BlockSpec((1, d), lambda i, j: (0, 0)),
                  pl.BlockSpec((d, tn), lambda i, j: (0, j))],
        out_specs=pl.BlockSpec((tm, tn), lambda i, j: (i, j)),
        out_shape=jax.ShapeDtypeStruct((t, n), f32),
        scratch_shapes=[pltpu.VMEM((tm, d), bf16)],
        compiler_params=_params("parallel", "arbitrary"),
        name="norm_proj",
    )(x, g.reshape(1, d), w_bf16)


def _mix_prep_body(proj_ref, before_ref, state_ref, mu_ref, gq_ref, gk_ref, w0_ref, w2_ref, a0_ref, a2_ref, g2_ref,
                   kk_ref, ka_ref, r1_ref, r2_ref,
                   q_ref, k_ref, r_ref, lw_ref, k2_ref, v_ref, a_ref, b_ref, g_ref,
                   *, prompt_tiles, prompt_len, sample_len):
    tm = proj_ref.shape[0]
    i = pl.program_id(0)
    seg_ones = r1_ref[...]
    seg_bcast = r2_ref[...]
    row = _iota((tm, 1), 0)
    is_sample = i >= prompt_tiles
    keep_first = jnp.where((i * tm) % prompt_len == 0, 0.0, 1.0)
    n_seq = tm // sample_len
    pick = ((_iota((tm, n_seq), 0) == _iota((tm, n_seq), 1) * sample_len)).astype(bf16)
    opens_sample = row % sample_len == 0

    def head_sum(x):
        return _mm(x, seg_ones, NN, 3, 1)

    def head_spread(s):
        return _mm(s, seg_bcast, NN, 3, 1)

    q = proj_ref[:, 0:D_GRP]
    qn = q * head_spread(lax.rsqrt(head_sum(q * q) * (1.0 / HEAD_DIM) + RMS_EPS))
    q_ref[...] = qn * gq_ref[...] * QK_SCALE
    k = proj_ref[:, D_GRP:2 * D_GRP]
    kn = k * head_spread(lax.rsqrt(head_sum(k * k) * (1.0 / HEAD_DIM) + RMS_EPS))
    k_ref[...] = kn * gk_ref[...]

    base = 3 * D_GRP

    def shifted(lo, hi):
        p = proj_ref[:, base + lo:base + hi]
        before = jnp.where(row == 0, before_ref[SUBLANES - 1:SUBLANES, base + lo:base + hi], pltpu.roll(p, 1, axis=0))
        in_prompt = jnp.where(row == 0, before * keep_first, before)
        in_sample = jnp.where(opens_sample, _mm(pick, state_ref[:, lo:hi], NN, 1, 3), before)
        prev = jnp.where(is_sample, in_sample, in_prompt)
        return p + (prev - p) * mu_ref[:, lo:hi]

    r = shifted(0, D_GRP)
    kr = shifted(D_GRP, 2 * D_GRP)
    v = shifted(2 * D_GRP, 3 * D_GRP)
    tail = shifted(3 * D_GRP, N_SHIFT)
    wd = tail[:, 0:R_DECAY]
    ad = tail[:, R_DECAY:R_DECAY + R_AAA]
    gd = tail[:, R_DECAY + R_AAA:]

    w_pre = w0_ref[...] + _mm(jnp.tanh(wd), w2_ref[...])
    w_log = -_softplus(-w_pre) - 0.5
    lw_ref[...] = -jnp.exp(w_log)
    a_lr = jax.nn.sigmoid(a0_ref[...] + _mm(ad, a2_ref[...]))
    g_ref[...] = _mm(jax.nn.sigmoid(gd), g2_ref[...])

    kk = kr * kk_ref[...]
    kk = kk * head_spread(lax.rsqrt(jnp.maximum(head_sum(kk * kk), 1e-24)))
    r_ref[...] = r
    k2_ref[...] = kr * (1.0 + (a_lr - 1.0) * ka_ref[...])
    v_ref[...] = v
    a_ref[...] = -kk
    b_ref[...] = kk * a_lr


def _mix_prep(proj, shift_state, n_prompt, prompt_len, sample_len, mu, gq, gk, w0, w2, a0, a2, g2, k_k, k_a, tm=128):
    t = proj.shape[0]
    heads = _iota((D_GRP, LANES), 0) // HEAD_DIM
    seg_ones = (heads == _iota((D_GRP, LANES), 1)).astype(bf16)
    seg_bcast = seg_ones.T
    prompt_tiles = n_prompt // tm
    n_seq = tm // sample_len
    last_state_blk = shift_state.shape[0] // n_seq - 1
    row = lambda n: pl.BlockSpec((1, n), lambda i: (0, 0))
    full = lambda a: pl.BlockSpec(a.shape, lambda i: (0, 0))
    tok = lambda n: pl.BlockSpec((tm, n), lambda i: (i, 0))
    before = pl.BlockSpec((SUBLANES, N_IN), lambda i: (jnp.maximum(i * (tm // SUBLANES) - 1, 0), 0))
    state = pl.BlockSpec((n_seq, N_SHIFT), lambda i: (jnp.clip(i - prompt_tiles, 0, last_state_blk), 0))
    outs = pl.pallas_call(
        functools.partial(_mix_prep_body, prompt_tiles=prompt_tiles, prompt_len=prompt_len, sample_len=sample_len),
        grid=(t // tm,),
        in_specs=[tok(N_IN), before, state, row(N_SHIFT), row(D_GRP), row(D_GRP), row(D_GRP), full(w2), row(D_GRP),
                  full(a2), full(g2), row(D_GRP), row(D_GRP), full(seg_ones), full(seg_bcast)],
        out_specs=[tok(D_GRP)] * 9,
        out_shape=[jax.ShapeDtypeStruct((t, D_GRP), f32)] * 9,
        compiler_params=_params("parallel"),
        name="mix_prep",
    )(proj, proj, shift_state, mu.reshape(1, -1), jnp.tile(gq, N_HEADS).reshape(1, -1), jnp.tile(gk, N_HEADS).reshape(1, -1),
      w0.reshape(1, -1), w2.astype(bf16), a0.reshape(1, -1), a2.astype(bf16), g2.astype(bf16),
      k_k.reshape(1, -1), k_a.reshape(1, -1), seg_ones, seg_bcast)
    return outs


def _suffix_matrix(tk):
    r = _iota((2 * tk, tk), 0) % tk
    c = _iota((2 * tk, tk), 1)
    return jnp.where(r > c, 1.0, 0.0).astype(bf16)


def _sb_scores(z, suffix_mat, mask):
    sp = jnp.maximum(z, 0.0) + jnp.log(1.0 + jnp.exp(-jnp.abs(z)))
    log_1m = -sp
    if mask is not None:
        log_1m = jnp.where(mask, log_1m, 0.0)
    hi = log_1m.astype(bf16)
    lo = (log_1m - hi.astype(f32)).astype(bf16)
    later = jnp.dot(jnp.concatenate([hi, lo], axis=1), suffix_mat, preferred_element_type=f32)
    return (z - sp) + later, jnp.sum(log_1m, axis=-1, keepdims=True)


def _sb_prompt_body(q_ref, k_ref, v_ref, bias_ref, sfx_ref, o_ref, acc_scr, *carry_scr, tq, tk, hp):
    i = pl.program_id(2)
    ratio = tq // tk
    hs = range(hp)
    lane_head = _iota((tq, hp * HEAD_DIM), 1) // HEAD_DIM
    first = _iota((tq, LANES), 1) < HEAD_DIM
    qall = q_ref[...]
    qh = [jnp.where(lane_head == h, qall, 0.0).astype(bf16) for h in hs]
    bias = [bias_ref[0, h:h + 1, :] for h in hs]
    sfx = sfx_ref[...]
    for h in hs:
        carry_scr[h][...] = jnp.zeros((tq, 1), f32)
    acc_scr[...] = jnp.zeros_like(acc_scr)
    row = _iota((tq, tk), 0)
    col = _iota((tq, tk), 1)

    def block(j, masked):
        ks = pl.ds(pl.multiple_of(j * tk, tk), tk)
        kb = k_ref[ks, :].astype(bf16)
        vb = v_ref[ks, :].astype(bf16)
        mask = (j * tk + col < i * tq + row) if masked else None
        zs = [_mm(qh[h], kb, NT) + bias[h] for h in hs]
        sc = [_sb_scores(z, sfx, mask) for z in zs]
        ws = []
        for h in hs:
            carry = carry_scr[h][...]
            a = jnp.exp(sc[h][0] + carry)
            if masked:
                a = jnp.where(mask, a, 0.0)
            carry_scr[h][...] = carry + sc[h][1]
            ws.append(a.astype(bf16))
        os_ = [_mm(ws[h], vb[:, (h // 2) * LANES:(h // 2 + 1) * LANES]) for h in hs]
        upd =[jnp.where(first, os_[2 * p], os_[2 * p + 1]) for p in range(hp // 2)]
        acc_scr[...] += jnp.concatenate(upd, axis=1)

    for u in range(ratio):
        block((i + 1) * ratio - 1 - u, True)

    def step(jj, c):
        block(i * ratio - 1 - jj, False)
        return c

    lax.fori_loop(0, i * ratio, step, 0)
    o_ref[...] = acc_scr[...]


def _sb_prompt(q, k, proj, bias, nb, t, tq=256, tk=PAGE, hp=4):
    groups = N_HEADS // hp
    width = hp * HEAD_DIM
    nq = t // tq
    bias_rows = jnp.broadcast_to(bias.astype(f32).reshape(groups, hp, 1), (groups, hp, tk))
    v_blk0 = 2 * D_GRP // width
    return pl.pallas_call(
        functools.partial(_sb_prompt_body, tq=tq, tk=tk, hp=hp),
        grid=(nb, groups, nq),
        in_specs=[pl.BlockSpec((tq, width), lambda b, p, i: (b * nq + i, p)),
                  pl.BlockSpec((t, width), lambda b, p, i: (b, p)),
                  pl.BlockSpec((t, width), lambda b, p, i: (b, v_blk0 + p)),
                  pl.BlockSpec((1, hp, tk), lambda b, p, i: (p, 0, 0)),
                  pl.BlockSpec((2 * tk, tk), lambda b, p, i: (0, 0))],
        out_specs=pl.BlockSpec((tq, width), lambda b, p, i: (b * nq + i, p)),
        out_shape=jax.ShapeDtypeStruct((nb * t, D_GRP), f32),
        scratch_shapes=[pltpu.VMEM((tq, width), f32)] + [pltpu.VMEM((tq, 1), f32)] * hp,
        compiler_params=_params("parallel", "parallel", "arbitrary"),
        name="sb_prompt",
    )(q, k, proj, bias_rows, _suffix_matrix(tk))


def _sb_sample_body(pt_ref, q_ref, kn_ref, vn_ref, bias_ref, sfx_ref, *refs, n_new, pp):
    kc_refs, vc_refs = refs[:pp], refs[pp:2 * pp]
    o_ref, q_scr, acc_scr, carry_scr = refs[2 * pp:]
    s = pl.program_id(1)
    rows = N_HEADS * n_new
    heads = range(N_HEADS)
    sfx = sfx_ref[...]
    bias = bias_ref[...]
    hrows = lambda x, h: x[h * n_new:(h + 1) * n_new]

    @pl.when(s == 0)
    def _():
        q = q_ref[...]
        for h in heads:
            q_scr[h] = _head(q, h)
        pad = jnp.zeros((PAGE - n_new, HEAD_DIM), f32)
        kn, vn = kn_ref[...], vn_ref[...]
        zs = [_mm(_head(q, h).astype(bf16), jnp.concatenate([_head(kn, h), pad], axis=0).astype(bf16), NT)
              for h in heads]
        causal = _iota((rows, PAGE), 1) < _iota((rows, PAGE), 0) % n_new
        lbs, tot = _sb_scores(jnp.concatenate(zs, axis=0) + bias, sfx, causal)
        a = jnp.where(causal, jnp.exp(lbs), 0.0)
        os_ = [_mm(hrows(a, h).astype(bf16), jnp.concatenate([_head(vn, h), pad], axis=0).astype(bf16))
               for h in heads]
        acc_scr[...] = jnp.concatenate(os_, axis=0)
        carry_scr[...] = tot

    qb = [q_scr[h].astype(bf16) for h in heads]
    zs = [jnp.concatenate([_mm(qb[h], kc_refs[u][0, 0, h].astype(bf16)) for h in heads], axis=0) + bias
          for u in range(pp)]
    sc = [_sb_scores(z, sfx, None) for z in zs]
    carry = carry_scr[...]
    ws = []
    for u in range(pp):
        ws.append(jnp.exp(sc[u][0] + carry))
        carry = carry + sc[u][1]
    carry_scr[...] = carry
    os_ = [[_mm(hrows(ws[u], h).astype(bf16), vc_refs[u][0, 0, h].astype(bf16), NT) for h in heads]
           for u in range(pp)]
    upd = [functools.reduce(lambda x, y: x + y, [os_[u][h] for u in range(pp)]) for h in heads]
    acc_scr[...] += jnp.concatenate(upd, axis=0)

    @pl.when(s == pl.num_programs(1) - 1)
    def _():
        acc = acc_scr[...]
        for h in heads:
            o_ref[0, h] = hrows(acc, h)


def _sb_sample(q, k, proj, bias, cache_k_t, cache_v_t, page_table, layer, row0, n_new, pp=8):
    nb, n_pages = page_table.shape
    rows = N_HEADS * n_new
    bias_rows = jnp.repeat(bias.astype(f32), n_new).reshape(rows, 1)
    blk0 = row0 // n_new
    v_blk0 = 2 * D_GRP // D_GRP

    def page(u):
        return lambda b, s, pt: (layer, pt[b, n_pages - 1 - (s * pp + u)], 0, 0, 0)

    tok = lambda cb: pl.BlockSpec((n_new, D_GRP), lambda b, s, pt: (blk0 + b, cb))
    page_specs = [pl.BlockSpec((1, 1, N_HEADS, HEAD_DIM, PAGE), page(u)) for u in range(pp)]
    grid_spec = pltpu.PrefetchScalarGridSpec(
        num_scalar_prefetch=1,
        grid=(nb, n_pages // pp),
        in_specs=[tok(0), tok(0), tok(v_blk0),
                  pl.BlockSpec((rows, 1), lambda b, s, pt: (0, 0)),
                  pl.BlockSpec((2 * PAGE, PAGE), lambda b, s, pt: (0, 0))] + page_specs + page_specs,
        out_specs=pl.BlockSpec((1, N_HEADS, n_new, HEAD_DIM), lambda b, s, pt: (b, 0, 0, 0)),
        scratch_shapes=[pltpu.VMEM((N_HEADS, n_new, HEAD_DIM), f32), pltpu.VMEM((rows, HEAD_DIM), f32),
                        pltpu.VMEM((rows, 1), f32)],
    )
    return pl.pallas_call(
        functools.partial(_sb_sample_body, n_new=n_new, pp=pp),
        grid_spec=grid_spec,
        out_shape=jax.ShapeDtypeStruct((nb, N_HEADS, n_new, HEAD_DIM), f32),
        compiler_params=_params("parallel", "arbitrary"),
        name="sb_sample",
    )(page_table, q, k, proj, bias_rows, _suffix_matrix(PAGE),
      *([cache_k_t] * pp), *([cache_v_t] * pp))


def _unit_lower_inverses(a_strict, c):
    r = _iota((c, c), 0)
    col = _iota((c, c), 1)
    mm = lambda x, y: _mm(x, y, NN, 2, 2)
    base = SUBLANES
    ps = [jnp.where(r // base == col // base, a, 0.0) for a in a_strict]
    eye = jnp.where(r == col, 1.0, 0.0)
    ts = [eye + p for p in ps]
    for _ in range(2):
        ps = [mm(p, p) for p in ps]
        ts = [t + mm(t, p) for t, p in zip(ts, ps)]
    s = base
    while s < c:
        join = (r // (2 * s) == col // (2 * s)) & (r // s != col // s)
        low = [mm(jnp.where(join, a, 0.0), t) for a, t in zip(a_strict, ts)]
        ts = [t + mm(t, x) for t, x in zip(ts, low)]
        s *= 2
    return ts


def _rwkv_body(*refs, c, n_chunks, has_init):
    if has_init:
        (r_ref, lw_ref, k_ref, v_ref, a_ref, b_ref, gate_ref, rk_ref, lng_ref, lnb_ref, s0_ref,
         o_ref, sout_ref, s_scr) = refs
    else:
        (r_ref, lw_ref, k_ref, v_ref, a_ref, b_ref, gate_ref, rk_ref, lng_ref, lnb_ref,
         o_ref, sout_ref, s_scr) = refs
    tstep = pl.program_id(1)
    heads = range(N_HEADS)

    @pl.when(tstep == 0)
    def _():
        s_scr[...] = s0_ref[0] if has_init else jnp.zeros_like(s_scr)

    row = _iota((c, c), 0)
    col = _iota((c, c), 1)
    incl = jnp.where(row >= col, 1.0, 0.0).astype(bf16)
    row2 = _iota((2 * c, 2 * c), 0)
    col2 = _iota((2 * c, 2 * c), 1)
    lower = (col2 % c) < jnp.where(row2 < c, row2, row2 - c + 1)
    mm = lambda x, y, dn=NN: _mm(x, y, dn, 2, 2)
    mm1 = lambda x, y, dn=NN: _mm(x, y, dn, 1, 1)
    stack = lambda x, y: jnp.concatenate([x, y], axis=0)

    def chunk(ci, carry):
        ts = pl.ds(pl.multiple_of(ci * c, c), c)
        r, lw, k, v = r_ref[ts, :], lw_ref[ts, :], k_ref[ts, :], v_ref[ts, :]
        cum = _mm(incl, lw, NN, 1, 3)
        grow = jnp.exp(-cum)
        a_t = a_ref[ts, :] * jnp.exp(cum - lw)
        r_t = r * jnp.exp(cum)
        b_t = b_ref[ts, :] * grow
        k_t = k * grow
        end = jnp.exp(cum[c - 1:c, :])
        left = [stack(_head(a_t, h), _head(r_t, h)) for h in heads]
        right = [stack(_head(b_t, h), _head(k_t, h)) for h in heads]
        vs = [_head(v, h) for h in heads]
        s0 = [s_scr[h] for h in heads]
        gram = [jnp.where(lower, mm(l, rt, NT), 0.0) for l, rt in zip(left, right)]
        from_state = [mm1(l, s, NT) for l, s in zip(left, s0)]
        t_inv = _unit_lower_inverses([g[:c, :c] for g in gram], c)
        akv = [mm1(g[:c, c:], x) for g, x in zip(gram, vs)]
        us = [mm(t, f[:c] + x) for t, f, x in zip(t_inv, from_state, akv)]
        uv = [stack(u, x) for u, x in zip(us, vs)]
        ys = [f[c:] + mm1(g[c:], x) for f, g, x in zip(from_state, gram, uv)]
        s_new = [(s + mm1(x, rt, TN)) * _head(end, h) for h, (s, x, rt) in enumerate(zip(s0, uv, right))]
        for h in heads:
            s_scr[h] = s_new[h]
            y = ys[h]
            mu = jnp.mean(y, axis=-1, keepdims=True)
            yc = y - mu
            var = jnp.mean(yc * yc, axis=-1, keepdims=True)
            yn = yc * lax.rsqrt(var + LN_X_EPS) * _head(lng_ref, h) + _head(lnb_ref, h)
            bonus = jnp.sum(_head(r, h) * _head(k, h) * _head(rk_ref, h), axis=-1, keepdims=True) * vs[h]
            o_ref[ts, h * HEAD_DIM:(h + 1) * HEAD_DIM] = (yn + bonus) * gate_ref[ts, h * HEAD_DIM:(h + 1) * HEAD_DIM]
        return carry

    lax.fori_loop(0, n_chunks, chunk, 0)

    @pl.when(tstep == pl.num_programs(1) - 1)
    def _():
        sout_ref[0] = s_scr[...]


def _rwkv_scan(rw_in, r_k, ln_g, ln_b, s0, nb, t, row0, c, t_blk):
    nt = t // t_blk
    blk0 = row0 // t_blk
    seq = pl.BlockSpec((t_blk, D_GRP), lambda b, ti: (blk0 + b * nt + ti, 0))
    per_lane = pl.BlockSpec((1, D_GRP), lambda b, ti: (0, 0))
    state = pl.BlockSpec((1, N_HEADS, HEAD_DIM, HEAD_DIM), lambda b, ti: (b, 0, 0, 0))
    has_init = s0 is not None
    args = list(rw_in) + [r_k.reshape(1, D_GRP), ln_g.reshape(1, D_GRP), ln_b.reshape(1, D_GRP)]
    in_specs = [seq] * 7 + [per_lane] * 3
    if has_init:
        args.append(s0)
        in_specs.append(state)
    return pl.pallas_call(
        functools.partial(_rwkv_body, c=c, n_chunks=t_blk // c, has_init=has_init),
        grid=(nb, nt),
        in_specs=in_specs,
        out_specs=[pl.BlockSpec((t_blk, D_GRP), lambda b, ti: (b * nt + ti, 0)), state],
        out_shape=[jax.ShapeDtypeStruct((nb * t, D_GRP), f32),
                   jax.ShapeDtypeStruct((nb, N_HEADS, HEAD_DIM, HEAD_DIM), f32)],
        scratch_shapes=[pltpu.VMEM((N_HEADS, HEAD_DIM, HEAD_DIM), f32)],
        compiler_params=_params("parallel", "arbitrary"),
        name="rwkv_scan",
    )(*args)


def _out_proj_body(x_ref, osb_p_ref, osb_s_ref, orw_p_ref, orw_s_ref, scale_ref, w_ref, o_ref, *, prompt_tiles):
    sample = pl.program_id(0) >= prompt_tiles
    o_sb = jnp.where(sample, osb_s_ref[...], osb_p_ref[...])
    o_rw = jnp.where(sample, orw_s_ref[...], orw_p_ref[...])
    acc = jnp.dot((o_sb * scale_ref[...]).astype(bf16), w_ref[0:D_GRP, :], preferred_element_type=f32)
    acc = acc + jnp.dot(o_rw.astype(bf16), w_ref[D_GRP:, :], preferred_element_type=f32)
    o_ref[...] = x_ref[...] + acc


def _out_proj(x, osb_p, osb_s, orw_p, orw_s, scale, w_bf16, tm=512):
    t, d = x.shape
    prompt_tiles = osb_p.shape[0] // tm
    prompt = pl.BlockSpec((tm, D_GRP), lambda i: (jnp.minimum(i, prompt_tiles - 1), 0))
    sample = pl.BlockSpec((tm, D_GRP), lambda i: (jnp.maximum(i - prompt_tiles, 0), 0))
    return pl.pallas_call(
        functools.partial(_out_proj_body, prompt_tiles=prompt_tiles),
        grid=(t // tm,),
        in_specs=[pl.BlockSpec((tm, d), lambda i: (i, 0)), prompt, sample, prompt, sample,
                  pl.BlockSpec((1, D_GRP), lambda i: (0, 0)),
                  pl.BlockSpec(w_bf16.shape, lambda i: (0, 0))],
        out_specs=pl.BlockSpec((tm, d), lambda i: (i, 0)),
        out_shape=jax.ShapeDtypeStruct((t, d), f32),
        compiler_params=_params("parallel"),
        name="out_proj",
    )(x, osb_p, osb_s, orw_p, orw_s, scale.reshape(1, -1), w_bf16)


def _rms_rows(x, g):
    return x * lax.rsqrt(jnp.mean(x * x, axis=-1, keepdims=True) + RMS_EPS) * g


def _ffn_body(x_ref, g_ref, wg_ref, wu_ref, wd_ref, o_ref, h_scr):
    f = pl.program_id(1)

    @pl.when(f == 0)
    def _():
        x = x_ref[...]
        h_scr[...] = _rms_rows(x, g_ref[...]).astype(bf16)
        o_ref[...] = x

    h = h_scr[...]
    act = jax.nn.silu(jnp.dot(h, wg_ref[...], preferred_element_type=f32)) * jnp.dot(h, wu_ref[...], preferred_element_type=f32)
    o_ref[...] += jnp.dot(act.astype(bf16), wd_ref[...], preferred_element_type=f32)


def _ffn(x, g, wg, wu, wd, tm=512, tf=512):
    t, d = x.shape
    dff = wg.shape[1]
    return pl.pallas_call(
        _ffn_body,
        grid=(t // tm, dff // tf),
        in_specs=[pl.BlockSpec((tm, d), lambda i, f: (i, 0)),
                  pl.BlockSpec((1, d), lambda i, f: (0, 0)),
                  pl.BlockSpec((d, tf), lambda i, f: (0, f)),
                  pl.BlockSpec((d, tf), lambda i, f: (0, f)),
                  pl.BlockSpec((tf, d), lambda i, f: (f, 0))],
        out_specs=pl.BlockSpec((tm, d), lambda i, f: (i, 0)),
        out_shape=jax.ShapeDtypeStruct((t, d), f32),
        scratch_shapes=[pltpu.VMEM((tm, d), bf16)],
        compiler_params=_params("parallel", "arbitrary"),
        name="ffn_dense",
    )(x, g.reshape(1, d), wg, wu, wd)


def _top2_gates(logits):
    lane = _iota(logits.shape, 1)
    valid = lane < N_EXPERTS
    m = jnp.max(jnp.where(valid, logits, -jnp.inf), axis=-1, keepdims=True)
    e = jnp.where(valid, jnp.exp(logits - m), 0.0)
    p = e / jnp.sum(e, axis=-1, keepdims=True)
    p1 = jnp.max(p, axis=-1, keepdims=True)
    i1 = jnp.min(jnp.where(valid & (p == p1), lane, LANES), axis=-1, keepdims=True)
    rest = jnp.where(valid & (lane != i1), p, -1.0)
    p2 = jnp.max(rest, axis=-1, keepdims=True)
    i2 = jnp.min(jnp.where(rest == p2, lane, LANES), axis=-1, keepdims=True)
    tot = p1 + p2
    gates = jnp.where(lane == i1, p1 / tot, jnp.where(lane == i2, p2 / tot, 0.0))
    chosen = (lane == i1 + N_EXPERTS) | (lane == i2 + N_EXPERTS)
    return jnp.where(chosen, 1.0, gates)


def _router_body(x_ref, g_ref, router_ref, h_ref, gates_ref):
    hn = _rms_rows(x_ref[...], g_ref[...])
    h_ref[...] = hn
    gates_ref[...] = _top2_gates(_mm(hn, router_ref[...], NN, 3, 3))


def _router(x, g, router, tm=512):
    t, d = x.shape
    router_pad = jnp.zeros((d, LANES), f32).at[:, :router.shape[1]].set(router)
    return pl.pallas_call(
        _router_body,
        grid=(t // tm,),
        in_specs=[pl.BlockSpec((tm, d), lambda i: (i, 0)),
                  pl.BlockSpec((1, d), lambda i: (0, 0)),
                  pl.BlockSpec((d, LANES), lambda i: (0, 0))],
        out_specs=[pl.BlockSpec((tm, d), lambda i: (i, 0)), pl.BlockSpec((tm, LANES), lambda i: (i, 0))],
        out_shape=[jax.ShapeDtypeStruct((t, d), f32), jax.ShapeDtypeStruct((t, LANES), f32)],
        compiler_params=_params("parallel"),
        name="moe_router",
    )(x, g.reshape(1, d), router_pad)


def _for_rows(n, fn):
    def octet(o, c):
        for u in range(SUBLANES):
            fn(o * SUBLANES + u)
        return c
    lax.fori_loop(0, n // SUBLANES, octet, 0)


def _dispatch_body(pos_ref, h_hbm, init_hbm, out_hbm, sem, *, tm):
    del init_hbm
    row0 = pl.program_id(0) * tm

    def copy(s, r):
        return pltpu.make_async_copy(h_hbm.at[pl.ds(row0 + r, 1), :], out_hbm.at[pl.ds(pos_ref[0, 0, s * tm + r], 1), :], sem)

    for s in range(2):
        _for_rows(tm, lambda r: copy(s, r).start())
    for s in range(2):
        _for_rows(tm, lambda r: copy(s, r).wait())


def _expert_rows_body(te_ref, nr_ref, x_ref, wg_ref, wu_ref, wd_ref, o_ref, xb, acc):
    i = pl.program_id(0)
    f = pl.program_id(1)
    used = nr_ref[i] > 0
    last = f == pl.num_programs(1) - 1

    @pl.when(used & (f == 0))
    def _():
        xb[...] = x_ref[...].astype(bf16)
        acc[...] = jnp.zeros_like(acc)

    @pl.when(used)
    def _():
        h = xb[...]
        act = jax.nn.silu(jnp.dot(h, wg_ref[0], preferred_element_type=f32)) * jnp.dot(h, wu_ref[0], preferred_element_type=f32)
        acc[...] += jnp.dot(act.astype(bf16), wd_ref[0], preferred_element_type=f32)

    @pl.when(used & last)
    def _():
        o_ref[...] = acc[...]

    @pl.when(jnp.logical_not(used) & last)
    def _():
        o_ref[...] = jnp.zeros_like(o_ref)


def _combine_body(pos_ref, x_ref, g_ref, y_hbm, o_ref, ybuf, sem, *, tc):
    def copy(s, r):
        return pltpu.make_async_copy(y_hbm.at[pl.ds(pos_ref[0, 0, s * tc + r], 1), :], ybuf.at[s, pl.ds(r, 1), :], sem)

    for s in range(2):
        _for_rows(tc, lambda r: copy(s, r).start())
    for s in range(2):
        _for_rows(tc, lambda r: copy(s, r).wait())
    g = g_ref[...]
    o_ref[...] = x_ref[...] + (ybuf[0] * g[:, 0:1] + ybuf[1] * g[:, 1:2])


def _moe(x, g, router, wg, wu, wd, tm=512, tf=256):
    t, d = x.shape
    ne, _, de = wg.shape
    h, gates = _router(x, g, router)

    sel = (gates[:, ne:2 * ne] > 0.0).astype(jnp.int32)
    cnt = sel.sum(axis=0)
    cnt_pad = (cnt + tm - 1) // tm * tm
    ends = jnp.cumsum(cnt_pad)
    rank = jnp.cumsum(sel, axis=0) - sel
    slot = jnp.cumsum(sel, axis=1) - sel
    n_rows = 2 * t + ne * tm
    n_tiles = n_rows // tm
    row_of = (ends - cnt_pad)[None, :] + rank
    by_slot = lambda v: jnp.stack([jnp.sum(jnp.where((sel > 0) & (slot == s), v, 0), axis=1) for s in range(2)], axis=1)
    pos = by_slot(row_of).astype(jnp.int32)
    slot_gate = jnp.zeros((t, LANES), f32).at[:, :2].set(by_slot(gates[:, :ne]))
    tile_row0 = jnp.arange(n_tiles, dtype=jnp.int32) * tm
    tile_expert = jnp.minimum(jnp.sum(tile_row0[:, None] >= ends[None, :], axis=1), ne - 1).astype(jnp.int32)
    tile_real = jnp.where(tile_row0 < ends[-1], cnt[tile_expert] - (tile_row0 - (ends - cnt_pad)[tile_expert]), 0)
    tile_real = jnp.clip(tile_real, 0, tm).astype(jnp.int32)
    tile_expert = tile_expert[jnp.minimum(jnp.arange(n_tiles), ends[-1] // tm - 1)]

    dma_params = pltpu.CompilerParams(dimension_semantics=("arbitrary",), vmem_limit_bytes=VMEM_LIMIT,
                                      disable_bounds_checks=True)
    pos_rows = lambda n: pl.BlockSpec((1, 1, 2 * n), lambda i: (i, 0, 0), memory_space=pltpu.SMEM)
    by_tile = lambda n: pos.reshape(t // n, n, 2).transpose(0, 2, 1).reshape(t // n, 1, 2 * n)
    xs = pl.pallas_call(
        functools.partial(_dispatch_body, tm=tm),
        grid=(t // tm,),
        in_specs=[pos_rows(tm), pl.BlockSpec(memory_space=pl.ANY), pl.BlockSpec(memory_space=pl.ANY)],
        out_specs=pl.BlockSpec(memory_space=pl.ANY),
        out_shape=jax.ShapeDtypeStruct((n_rows, d), f32),
        scratch_shapes=[pltpu.SemaphoreType.DMA(())],
        input_output_aliases={2: 0},
        compiler_params=dma_params,
        name="moe_dispatch",
    )(by_tile(tm), h, jnp.zeros((n_rows, d), f32))

    nf = de // tf
    fblk = lambda i, f, te, nr: jnp.where(nr[i] > 0, f, nf - 1)
    grid_spec = pltpu.PrefetchScalarGridSpec(
        num_scalar_prefetch=2,
        grid=(n_tiles, nf),
        in_specs=[pl.BlockSpec((tm, d), lambda i, f, te, nr: (i, 0)),
                  pl.BlockSpec((1, d, tf), lambda i, f, te, nr: (te[i], 0, fblk(i, f, te, nr))),
                  pl.BlockSpec((1, d, tf), lambda i, f, te, nr: (te[i], 0, fblk(i, f, te, nr))),
                  pl.BlockSpec((1, tf, d), lambda i, f, te, nr: (te[i], fblk(i, f, te, nr), 0))],
        out_specs=pl.BlockSpec((tm, d), lambda i, f, te, nr: (i, 0)),
        scratch_shapes=[pltpu.VMEM((tm, d), bf16), pltpu.VMEM((tm, d), f32)],
    )
    y = pl.pallas_call(
        _expert_rows_body,
        grid_spec=grid_spec,
        out_shape=jax.ShapeDtypeStruct((n_rows, d), f32),
        compiler_params=_params("parallel", "arbitrary"),
        name="moe_experts",
    )(tile_expert, tile_real, xs, wg, wu, wd)

    tc = 256
    return pl.pallas_call(
        functools.partial(_combine_body, tc=tc),
        grid=(t // tc,),
        in_specs=[pos_rows(tc),
                  pl.BlockSpec((tc, d), lambda i: (i, 0)),
                  pl.BlockSpec((tc, LANES), lambda i: (i, 0)),
                  pl.BlockSpec(memory_space=pl.ANY)],
        out_specs=pl.BlockSpec((tc, d), lambda i: (i, 0)),
        out_shape=jax.ShapeDtypeStruct((t, d), f32),
        scratch_shapes=[pltpu.VMEM((2, tc, d), f32), pltpu.SemaphoreType.DMA(())],
        compiler_params=dma_params,
        name="moe_combine",
    )(by_tile(tc), x, slot_gate, y)


def kernel(x_prompt, x_sample, cache_k, cache_v, page_table, state_wkv, state_shift, norm_mix_g, w_in, sb_q_norm_g, sb_k_norm_g, sb_logit_bias, sb_out_scale, rw_shift_mu, rw_w0, rw_w2, rw_a0, rw_a2, rw_g2, rw_k_k, rw_k_a, rw_r_k, rw_ln_g, rw_ln_b, w_out, norm_ffn_g, ffn_w_gate, ffn_w_up, ffn_w_down, moe_router, moe_w_gate, moe_w_up, moe_w_down):
    bp, tp, d = x_prompt.shape
    nb, ts, _ = x_sample.shape
    depth = w_in.shape[0]
    n_p = bp * tp
    x = jnp.concatenate([x_prompt.reshape(n_p, d), x_sample.reshape(nb * ts, d)], axis=0)
    cache_k_t = cache_k.transpose(0, 1, 3, 4, 2)
    cache_v_t = cache_v.transpose(0, 1, 3, 4, 2)
    rw0 = 3 * D_GRP

    kp, vp, ksm, vsm, wp, wsm, sp, ssm = [], [], [], [], [], [], [], []
    for l in range(depth):
        proj = _norm_proj(x, norm_mix_g[l], w_in[l].astype(bf16))
        q, k, r, lw, k2, v_rw, a, b, gate = _mix_prep(
            proj, state_shift[l], n_p, tp, ts, rw_shift_mu[l], sb_q_norm_g[l], sb_k_norm_g[l], rw_w0[l], rw_w2[l],
            rw_a0[l], rw_a2[l], rw_g2[l], rw_k_k[l], rw_k_a[l])
        v = proj[:, 2 * D_GRP:3 * D_GRP]

        o_sb_p = _sb_prompt(q, k, proj, sb_logit_bias[l], bp, tp)
        o_sb_s = _sb_sample(q, k, proj, sb_logit_bias[l], cache_k_t, cache_v_t, page_table, l, n_p, ts)
        o_sb_s = o_sb_s.transpose(0, 2, 1, 3).reshape(nb * ts, D_GRP)

        rw_in = (r, lw, k2, v_rw, a, b, gate)
        o_rw_p, wkv_p = _rwkv_scan(rw_in, rw_r_k[l], rw_ln_g[l], rw_ln_b[l], None, bp, tp, 0, c=64, t_blk=256)
        o_rw_s, wkv_s = _rwkv_scan(rw_in, rw_r_k[l], rw_ln_g[l], rw_ln_b[l], state_wkv[l], nb, ts, n_p, c=ts, t_blk=ts)

        x = _out_proj(x, o_sb_p, o_sb_s, o_rw_p, o_rw_s, sb_out_scale[l], w_out[l].astype(bf16))
        i = l // 2
        if l % 2 == 0:
            x = _ffn(x, norm_ffn_g[l], ffn_w_gate[i].astype(bf16), ffn_w_up[i].astype(bf16), ffn_w_down[i].astype(bf16))
        else:
            x = _moe(x, norm_ffn_g[l], moe_router[i], moe_w_gate[i].astype(bf16), moe_w_up[i].astype(bf16),
                     moe_w_down[i].astype(bf16))

        kp.append(k[:n_p].reshape(bp, tp, N_HEADS, HEAD_DIM))
        vp.append(v[:n_p].reshape(bp, tp, N_HEADS, HEAD_DIM))
        ksm.append(k[n_p:].reshape(nb, ts, N_HEADS, HEAD_DIM))
        vsm.append(v[n_p:].reshape(nb, ts, N_HEADS, HEAD_DIM))
        wp.append(wkv_p)
        wsm.append(wkv_s)
        sp.append(proj[tp - 1:n_p:tp, rw0:])
        ssm.append(proj[n_p + ts - 1::ts, rw0:])

    return (x[:n_p].reshape(bp, tp, d), x[n_p:].reshape(nb, ts, d), jnp.stack(kp), jnp.stack(vp), jnp.stack(ksm),
            jnp.stack(vsm), jnp.stack(wp), jnp.stack(wsm), jnp.stack(sp), jnp.stack(ssm))
```

```python
import functools

import jax
import jax.numpy as jnp
from jax import lax
from jax.experimental import pallas as pl
from jax.experimental.pallas import tpu as pltpu

f32 = jnp.float32
bf16 = jnp.bfloat16

D_MODEL = 2048
HEAD_DIM = 64
N_HEADS = 16
D_GRP = N_HEADS * HEAD_DIM
R_DECAY = 64
R_AAA = 64
R_GATE = 128
N_SHIFT = 3 * D_GRP + R_DECAY + R_AAA + R_GATE
N_IN = 3 * D_GRP + N_SHIFT
N_EXPERTS = 8
PAGE = 128
RMS_EPS = 1e-6
LN_X_EPS = 64e-5
QK_SCALE = HEAD_DIM ** -0.5

LANES = 128
SUBLANES = 8
VMEM_LIMIT = 56 * 1024 * 1024

NN = (((1,), (0,)), ((), ()))
NT = (((1,), (1,)), ((), ()))
TN = (((0,), (0,)), ((), ()))


def _params(*sem):
    return pltpu.CompilerParams(dimension_semantics=sem, vmem_limit_bytes=VMEM_LIMIT)


def _split(x, n):
    out, r = [], x
    for i in range(n):
        h = r.astype(bf16)
        out.append(h)
        if i + 1 < n:
            r = r - h.astype(f32)
    return out


def _mm(a, b, dn=NN, pa=1, pb=1):
    sa = [a] if a.dtype == bf16 else _split(a, pa)
    sb = [b] if b.dtype == bf16 else _split(b, pb)
    order = max(len(sa), len(sb))
    acc = None
    for i, ai in enumerate(sa):
        for j, bj in enumerate(sb):
            if i + j < order:
                t = lax.dot_general(ai, bj, dn, preferred_element_type=f32)
                acc = t if acc is None else acc + t
    return acc


def _softplus(z):
    return jnp.maximum(z, 0.0) + jnp.log1p(jnp.exp(-jnp.abs(z)))


def _iota(shape, dim):
    return lax.broadcasted_iota(jnp.int32, shape, dim)


def _head(x, h):
    return x[:, h * HEAD_DIM:(h + 1) * HEAD_DIM]


def _norm_proj_body(x_ref, g_ref, w_ref, o_ref, h_scr):
    @pl.when(pl.program_id(1) == 0)
    def _():
        x = x_ref[...]
        y = x * lax.rsqrt(jnp.mean(x * x, axis=-1, keepdims=True) + RMS_EPS)
        h_scr[...] = (y * g_ref[...]).astype(bf16)

    o_ref[...] = jnp.dot(h_scr[...], w_ref[...], preferred_element_type=f32)


def _norm_proj(x, g, w_bf16, tm=512, tn=1280):
    t, d = x.shape
    n = w_bf16.shape[1]
    return pl.pallas_call(
        _norm_proj_body,
        grid=(t // tm, n // tn),
        in_specs=[pl.BlockSpec((tm, d), lambda i, j: (i, 0)),
                  pl.BlockSpec((1, d), lambda i, j: (0, 0)),
                  pl.BlockSpec((d, tn), lambda i, j: (0, j))],
        out_specs=pl.BlockSpec((tm, tn), lambda i, j: (i, j)),
        out_shape=jax.ShapeDtypeStruct((t, n), f32),
        scratch_shapes=[pltpu.VMEM((tm, d), bf16)],
        compiler_params=_params("parallel", "arbitrary"),
        name="norm_proj",
    )(x, g.reshape(1, d), w_bf16)


def _mix_prep_body(proj_ref, before_ref, state_ref, mu_ref, gq_ref, gk_ref, w0_ref, w2_ref, a0_ref, a2_ref, g2_ref,
                   kk_ref, ka_ref, r1_ref, r2_ref,
                   q_ref, k_ref, r_ref, lw_ref, k2_ref, v_ref, a_ref, b_ref, g_ref,
                   *, prompt_tiles, prompt_len, sample_len):
    tm = proj_ref.shape[0]
    i = pl.program_id(0)
    seg_ones = r1_ref[...]
    seg_bcast = r2_ref[...]
    row = _iota((tm, 1), 0)
    is_sample = i >= prompt_tiles
    keep_first = jnp.where((i * tm) % prompt_len == 0, 0.0, 1.0)
    n_seq = tm // sample_len
    pick = ((_iota((tm, n_seq), 0) == _iota((tm, n_seq), 1) * sample_len)).astype(bf16)
    opens_sample = row % sample_len == 0

    def head_sum(x):
        return _mm(x, seg_ones, NN, 3, 1)

    def head_spread(s):
        return _mm(s, seg_bcast, NN, 3, 1)

    q = proj_ref[:, 0:D_GRP]
    qn = q * head_spread(lax.rsqrt(head_sum(q * q) * (1.0 / HEAD_DIM) + RMS_EPS))
    q_ref[...] = qn * gq_ref[...] * QK_SCALE
    k = proj_ref[:, D_GRP:2 * D_GRP]
    kn = k * head_spread(lax.rsqrt(head_sum(k * k) * (1.0 / HEAD_DIM) + RMS_EPS))
    k_ref[...] = kn * gk_ref[...]

    base = 3 * D_GRP

    def shifted(lo, hi):
        p = proj_ref[:, base + lo:base + hi]
        before = jnp.where(row == 0, before_ref[SUBLANES - 1:SUBLANES, base + lo:base + hi], pltpu.roll(p, 1, axis=0))
        in_prompt = jnp.where(row == 0, before * keep_first, before)
        in_sample = jnp.where(opens_sample, _mm(pick, state_ref[:, lo:hi], NN, 1, 3), before)
        prev = jnp.where(is_sample, in_sample, in_prompt)
        return p + (prev - p) * mu_ref[:, lo:hi]

    r = shifted(0, D_GRP)
    kr = shifted(D_GRP, 2 * D_GRP)
    v = shifted(2 * D_GRP, 3 * D_GRP)
    tail = shifted(3 * D_GRP, N_SHIFT)
    wd = tail[:, 0:R_DECAY]
    ad = tail[:, R_DECAY:R_DECAY + R_AAA]
    gd = tail[:, R_DECAY + R_AAA:]

    w_pre = w0_ref[...] + _mm(jnp.tanh(wd), w2_ref[...])
    w_log = -_softplus(-w_pre) - 0.5
    lw_ref[...] = -jnp.exp(w_log)
    a_lr = jax.nn.sigmoid(a0_ref[...] + _mm(ad, a2_ref[...]))
    g_ref[...] = _mm(jax.nn.sigmoid(gd), g2_ref[...])

    kk = kr * kk_ref[...]
    kk = kk * head_spread(lax.rsqrt(jnp.maximum(head_sum(kk * kk), 1e-24)))
    r_ref[...] = r
    k2_ref[...] = kr * (1.0 + (a_lr - 1.0) * ka_ref[...])
    v_ref[...] = v
    a_ref[...] = -kk
    b_ref[...] = kk * a_lr


def _mix_prep(proj, shift_state, n_prompt, prompt_len, sample_len, mu, gq, gk, w0, w2, a0, a2, g2, k_k, k_a, tm=128):
    t = proj.shape[0]
    heads = _iota((D_GRP, LANES), 0) // HEAD_DIM
    seg_ones = (heads == _iota((D_GRP, LANES), 1)).astype(bf16)
    seg_bcast = seg_ones.T
    prompt_tiles = n_prompt // tm
    n_seq = tm // sample_len
    last_state_blk = shift_state.shape[0] // n_seq - 1
    row = lambda n: pl.BlockSpec((1, n), lambda i: (0, 0))
    full = lambda a: pl.BlockSpec(a.shape, lambda i: (0, 0))
    tok = lambda n: pl.BlockSpec((tm, n), lambda i: (i, 0))
    before = pl.BlockSpec((SUBLANES, N_IN), lambda i: (jnp.maximum(i * (tm // SUBLANES) - 1, 0), 0))
    state = pl.BlockSpec((n_seq, N_SHIFT), lambda i: (jnp.clip(i - prompt_tiles, 0, last_state_blk), 0))
    outs = pl.pallas_call(
        functools.partial(_mix_prep_body, prompt_tiles=prompt_tiles, prompt_len=prompt_len, sample_len=sample_len),
        grid=(t // tm,),
        in_specs=[tok(N_IN), before, state, row(N_SHIFT), row(D_GRP), row(D_GRP), row(D_GRP), full(w2), row(D_GRP),
                  full(a2), full(g2), row(D_GRP), row(D_GRP), full(seg_ones), full(seg_bcast)],
        out_specs=[tok(D_GRP)] * 9,
        out_shape=[jax.ShapeDtypeStruct((t, D_GRP), f32)] * 9,
        compiler_params=_params("parallel"),
        name="mix_prep",
    )(proj, proj, shift_state, mu.reshape(1, -1), jnp.tile(gq, N_HEADS).reshape(1, -1), jnp.tile(gk, N_HEADS).reshape(1, -1),
      w0.reshape(1, -1), w2.astype(bf16), a0.reshape(1, -1), a2.astype(bf16), g2.astype(bf16),
      k_k.reshape(1, -1), k_a.reshape(1, -1), seg_ones, seg_bcast)
    return outs


def _suffix_matrix(tk):
    r = _iota((2 * tk, tk), 0) % tk
    c = _iota((2 * tk, tk), 1)
    return jnp.where(r > c, 1.0, 0.0).astype(bf16)


def _sb_scores(z, suffix_mat, mask):
    sp = jnp.maximum(z, 0.0) + jnp.log(1.0 + jnp.exp(-jnp.abs(z)))
    log_1m = -sp
    if mask is not None:
        log_1m = jnp.where(mask, log_1m, 0.0)
    hi = log_1m.astype(bf16)
    lo = (log_1m - hi.astype(f32)).astype(bf16)
    later = jnp.dot(jnp.concatenate([hi, lo], axis=1), suffix_mat, preferred_element_type=f32)
    return (z - sp) + later, jnp.sum(log_1m, axis=-1, keepdims=True)


def _sb_prompt_body(q_ref, k_ref, v_ref, bias_ref, sfx_ref, o_ref, acc_scr, *carry_scr, tq, tk, hp):
    i = pl.program_id(2)
    ratio = tq // tk
    hs = range(hp)
    lane_head = _iota((tq, hp * HEAD_DIM), 1) // HEAD_DIM
    first = _iota((tq, LANES), 1) < HEAD_DIM
    qall = q_ref[...]
    qh = [jnp.where(lane_head == h, qall, 0.0).astype(bf16) for h in hs]
    bias = [bias_ref[0, h:h + 1, :] for h in hs]
    sfx = sfx_ref[...]
    for h in hs:
        carry_scr[h][...] = jnp.zeros((tq, 1), f32)
    acc_scr[...] = jnp.zeros_like(acc_scr)
    row = _iota((tq, tk), 0)
    col = _iota((tq, tk), 1)

    def block(j, masked):
        ks = pl.ds(pl.multiple_of(j * tk, tk), tk)
        kb = k_ref[ks, :].astype(bf16)
        vb = v_ref[ks, :].astype(bf16)
        mask = (j * tk + col < i * tq + row) if masked else None
        zs = [_mm(qh[h], kb, NT) + bias[h] for h in hs]
        sc = [_sb_scores(z, sfx, mask) for z in zs]
        ws = []
        for h in hs:
            carry = carry_scr[h][...]
            a = jnp.exp(sc[h][0] + carry)
            if masked:
                a = jnp.where(mask, a, 0.0)
            carry_scr[h][...] = carry + sc[h][1]
            ws.append(a.astype(bf16))
        os_ = [_mm(ws[h], vb[:, (h // 2) * LANES:(h // 2 + 1) * LANES]) for h in hs]
        upd =[jnp.where(first, os_[2 * p], os_[2 * p + 1]) for p in range(hp // 2)]
        acc_scr[...] += jnp.concatenate(upd, axis=1)

    for u in range(ratio):
        block((i + 1) * ratio - 1 - u, True)

    def step(jj, c):
        block(i * ratio - 1 - jj, False)
        return c

    lax.fori_loop(0, i * ratio, step, 0)
    o_ref[...] = acc_scr[...]


def _sb_prompt(q, k, proj, bias, nb, t, tq=256, tk=PAGE, hp=4):
    groups = N_HEADS // hp
    width = hp * HEAD_DIM
    nq = t // tq
    bias_rows = jnp.broadcast_to(bias.astype(f32).reshape(groups, hp, 1), (groups, hp, tk))
    v_blk0 = 2 * D_GRP // width
    return pl.pallas_call(
        functools.partial(_sb_prompt_body, tq=tq, tk=tk, hp=hp),
        grid=(nb, groups, nq),
        in_specs=[pl.BlockSpec((tq, width), lambda b, p, i: (b * nq + i, p)),
                  pl.BlockSpec((t, width), lambda b, p, i: (b, p)),
                  pl.BlockSpec((t, width), lambda b, p, i: (b, v_blk0 + p)),
                  pl.BlockSpec((1, hp, tk), lambda b, p, i: (p, 0, 0)),
                  pl.BlockSpec((2 * tk, tk), lambda b, p, i: (0, 0))],
        out_specs=pl.BlockSpec((tq, width), lambda b, p, i: (b * nq + i, p)),
        out_shape=jax.ShapeDtypeStruct((nb * t, D_GRP), f32),
        scratch_shapes=[pltpu.VMEM((tq, width), f32)] + [pltpu.VMEM((tq, 1), f32)] * hp,
        compiler_params=_params("parallel", "parallel", "arbitrary"),
        name="sb_prompt",
    )(q, k, proj, bias_rows, _suffix_matrix(tk))


def _sb_sample_body(pt_ref, q_ref, kn_ref, vn_ref, bias_ref, sfx_ref, *refs, n_new, pp):
    kc_refs, vc_refs = refs[:pp], refs[pp:2 * pp]
    o_ref, q_scr, acc_scr, carry_scr = refs[2 * pp:]
    s = pl.program_id(1)
    rows = N_HEADS * n_new
    heads = range(N_HEADS)
    sfx = sfx_ref[...]
    bias = bias_ref[...]
    hrows = lambda x, h: x[h * n_new:(h + 1) * n_new]

    @pl.when(s == 0)
    def _():
        q = q_ref[...]
        for h in heads:
            q_scr[h] = _head(q, h)
        pad = jnp.zeros((PAGE - n_new, HEAD_DIM), f32)
        kn, vn = kn_ref[...], vn_ref[...]
        zs = [_mm(_head(q, h).astype(bf16), jnp.concatenate([_head(kn, h), pad], axis=0).astype(bf16), NT)
              for h in heads]
        causal = _iota((rows, PAGE), 1) < _iota((rows, PAGE), 0) % n_new
        lbs, tot = _sb_scores(jnp.concatenate(zs, axis=0) + bias, sfx, causal)
        a = jnp.where(causal, jnp.exp(lbs), 0.0)
        os_ = [_mm(hrows(a, h).astype(bf16), jnp.concatenate([_head(vn, h), pad], axis=0).astype(bf16))
               for h in heads]
        acc_scr[...] = jnp.concatenate(os_, axis=0)
        carry_scr[...] = tot

    qb = [q_scr[h].astype(bf16) for h in heads]
    zs = [jnp.concatenate([_mm(qb[h], kc_refs[u][0, 0, h].astype(bf16)) for h in heads], axis=0) + bias
          for u in range(pp)]
    sc = [_sb_scores(z, sfx, None) for z in zs]
    carry = carry_scr[...]
    ws = []
    for u in range(pp):
        ws.append(jnp.exp(sc[u][0] + carry))
        carry = carry + sc[u][1]
    carry_scr[...] = carry
    os_ = [[_mm(hrows(ws[u], h).astype(bf16), vc_refs[u][0, 0, h].astype(bf16), NT) for h in heads]
           for u in range(pp)]
    upd = [functools.reduce(lambda x, y: x + y, [os_[u][h] for u in range(pp)]) for h in heads]
    acc_scr[...] += jnp.concatenate(upd, axis=0)

    @pl.when(s == pl.num_programs(1) - 1)
    def _():
        acc = acc_scr[...]
        for h in heads:
            o_ref[0, h] = hrows(acc, h)


def _sb_sample(q, k, proj, bias, cache_k_t, cache_v_t, page_table, layer, row0, n_new, pp=8):
    nb, n_pages = page_table.shape
    rows = N_HEADS * n_new
    bias_rows = jnp.repeat(bias.astype(f32), n_new).reshape(rows, 1)
    blk0 = row0 // n_new
    v_blk0 = 2 * D_GRP // D_GRP

    def page(u):
        return lambda b, s, pt: (layer, pt[b, n_pages - 1 - (s * pp + u)], 0, 0, 0)

    tok = lambda cb: pl.BlockSpec((n_new, D_GRP), lambda b, s, pt: (blk0 + b, cb))
    page_specs = [pl.BlockSpec((1, 1, N_HEADS, HEAD_DIM, PAGE), page(u)) for u in range(pp)]
    grid_spec = pltpu.PrefetchScalarGridSpec(
        num_scalar_prefetch=1,
        grid=(nb, n_pages // pp),
        in_specs=[tok(0), tok(0), tok(v_blk0),
                  pl.BlockSpec((rows, 1), lambda b, s, pt: (0, 0)),
                  pl.BlockSpec((2 * PAGE, PAGE), lambda b, s, pt: (0, 0))] + page_specs + page_specs,
        out_specs=pl.BlockSpec((1, N_HEADS, n_new, HEAD_DIM), lambda b, s, pt: (b, 0, 0, 0)),
        scratch_shapes=[pltpu.VMEM((N_HEADS, n_new, HEAD_DIM), f32), pltpu.VMEM((rows, HEAD_DIM), f32),
                        pltpu.VMEM((rows, 1), f32)],
    )
    return pl.pallas_call(
        functools.partial(_sb_sample_body, n_new=n_new, pp=pp),
        grid_spec=grid_spec,
        out_shape=jax.ShapeDtypeStruct((nb, N_HEADS, n_new, HEAD_DIM), f32),
        compiler_params=_params("parallel", "arbitrary"),
        name="sb_sample",
    )(page_table, q, k, proj, bias_rows, _suffix_matrix(PAGE),
      *([cache_k_t] * pp), *([cache_v_t] * pp))


def _unit_lower_inverses(a_strict, c):
    r = _iota((c, c), 0)
    col = _iota((c, c), 1)
    mm = lambda x, y: _mm(x, y, NN, 2, 2)
    base = SUBLANES
    ps = [jnp.where(r // base == col // base, a, 0.0) for a in a_strict]
    eye = jnp.where(r == col, 1.0, 0.0)
    ts = [eye + p for p in ps]
    for _ in range(2):
        ps = [mm(p, p) for p in ps]
        ts = [t + mm(t, p) for t, p in zip(ts, ps)]
    s = base
    while s < c:
        join = (r // (2 * s) == col // (2 * s)) & (r // s != col // s)
        low = [mm(jnp.where(join, a, 0.0), t) for a, t in zip(a_strict, ts)]
        ts = [t + mm(t, x) for t, x in zip(ts, low)]
        s *= 2
    return ts


def _rwkv_body(*refs, c, n_chunks, has_init):
    if has_init:
        (r_ref, lw_ref, k_ref, v_ref, a_ref, b_ref, gate_ref, rk_ref, lng_ref, lnb_ref, s0_ref,
         o_ref, sout_ref, s_scr) = refs
    else:
        (r_ref, lw_ref, k_ref, v_ref, a_ref, b_ref, gate_ref, rk_ref, lng_ref, lnb_ref,
         o_ref, sout_ref, s_scr) = refs
    tstep = pl.program_id(1)
    heads = range(N_HEADS)

    @pl.when(tstep == 0)
    def _():
        s_scr[...] = s0_ref[0] if has_init else jnp.zeros_like(s_scr)

    row = _iota((c, c), 0)
    col = _iota((c, c), 1)
    incl = jnp.where(row >= col, 1.0, 0.0).astype(bf16)
    row2 = _iota((2 * c, 2 * c), 0)
    col2 = _iota((2 * c, 2 * c), 1)
    lower = (col2 % c) < jnp.where(row2 < c, row2, row2 - c + 1)
    mm = lambda x, y, dn=NN: _mm(x, y, dn, 2, 2)
    mm1 = lambda x, y, dn=NN: _mm(x, y, dn, 1, 1)
    stack = lambda x, y: jnp.concatenate([x, y], axis=0)

    def chunk(ci, carry):
        ts = pl.ds(pl.multiple_of(ci * c, c), c)
        r, lw, k, v = r_ref[ts, :], lw_ref[ts, :], k_ref[ts, :], v_ref[ts, :]
        cum = _mm(incl, lw, NN, 1, 3)
        grow = jnp.exp(-cum)
        a_t = a_ref[ts, :] * jnp.exp(cum - lw)
        r_t = r * jnp.exp(cum)
        b_t = b_ref[ts, :] * grow
        k_t = k * grow
        end = jnp.exp(cum[c - 1:c, :])
        left = [stack(_head(a_t, h), _head(r_t, h)) for h in heads]
        right = [stack(_head(b_t, h), _head(k_t, h)) for h in heads]
        vs = [_head(v, h) for h in heads]
        s0 = [s_scr[h] for h in heads]
        gram = [jnp.where(lower, mm(l, rt, NT), 0.0) for l, rt in zip(left, right)]
        from_state = [mm1(l, s, NT) for l, s in zip(left, s0)]
        t_inv = _unit_lower_inverses([g[:c, :c] for g in gram], c)
        akv = [mm1(g[:c, c:], x) for g, x in zip(gram, vs)]
        us = [mm(t, f[:c] + x) for t, f, x in zip(t_inv, from_state, akv)]
        uv = [stack(u, x) for u, x in zip(us, vs)]
        ys = [f[c:] + mm1(g[c:], x) for f, g, x in zip(from_state, gram, uv)]
        s_new = [(s + mm1(x, rt, TN)) * _head(end, h) for h, (s, x, rt) in enumerate(zip(s0, uv, right))]
        for h in heads:
            s_scr[h] = s_new[h]
            y = ys[h]
            mu = jnp.mean(y, axis=-1, keepdims=True)
            yc = y - mu
            var = jnp.mean(yc * yc, axis=-1, keepdims=True)
            yn = yc * lax.rsqrt(var + LN_X_EPS) * _head(lng_ref, h) + _head(lnb_ref, h)
            bonus = jnp.sum(_head(r, h) * _head(k, h) * _head(rk_ref, h), axis=-1, keepdims=True) * vs[h]
            o_ref[ts, h * HEAD_DIM:(h + 1) * HEAD_DIM] = (yn + bonus) * gate_ref[ts, h * HEAD_DIM:(h + 1) * HEAD_DIM]
        return carry

    lax.fori_loop(0, n_chunks, chunk, 0)

    @pl.when(tstep == pl.num_programs(1) - 1)
    def _():
        sout_ref[0] = s_scr[...]


def _rwkv_scan(rw_in, r_k, ln_g, ln_b, s0, nb, t, row0, c, t_blk):
    nt = t // t_blk
    blk0 = row0 // t_blk
    seq = pl.BlockSpec((t_blk, D_GRP), lambda b, ti: (blk0 + b * nt + ti, 0))
    per_lane = pl.BlockSpec((1, D_GRP), lambda b, ti: (0, 0))
    state = pl.BlockSpec((1, N_HEADS, HEAD_DIM, HEAD_DIM), lambda b, ti: (b, 0, 0, 0))
    has_init = s0 is not None
    args = list(rw_in) + [r_k.reshape(1, D_GRP), ln_g.reshape(1, D_GRP), ln_b.reshape(1, D_GRP)]
    in_specs = [seq] * 7 + [per_lane] * 3
    if has_init:
        args.append(s0)
        in_specs.append(state)
    return pl.pallas_call(
        functools.partial(_rwkv_body, c=c, n_chunks=t_blk // c, has_init=has_init),
        grid=(nb, nt),
        in_specs=in_specs,
        out_specs=[pl.BlockSpec((t_blk, D_GRP), lambda b, ti: (b * nt + ti, 0)), state],
        out_shape=[jax.ShapeDtypeStruct((nb * t, D_GRP), f32),
                   jax.ShapeDtypeStruct((nb, N_HEADS, HEAD_DIM, HEAD_DIM), f32)],
        scratch_shapes=[pltpu.VMEM((N_HEADS, HEAD_DIM, HEAD_DIM), f32)],
        compiler_params=_params("parallel", "arbitrary"),
        name="rwkv_scan",
    )(*args)


def _out_proj_body(x_ref, osb_p_ref, osb_s_ref, orw_p_ref, orw_s_ref, scale_ref, w_ref, o_ref, *, prompt_tiles):
    sample = pl.program_id(0) >= prompt_tiles
    o_sb = jnp.where(sample, osb_s_ref[...], osb_p_ref[...])
    o_rw = jnp.where(sample, orw_s_ref[...], orw_p_ref[...])
    acc = jnp.dot((o_sb * scale_ref[...]).astype(bf16), w_ref[0:D_GRP, :], preferred_element_type=f32)
    acc = acc + jnp.dot(o_rw.astype(bf16), w_ref[D_GRP:, :], preferred_element_type=f32)
    o_ref[...] = x_ref[...] + acc


def _out_proj(x, osb_p, osb_s, orw_p, orw_s, scale, w_bf16, tm=512):
    t, d = x.shape
    prompt_tiles = osb_p.shape[0] // tm
    prompt = pl.BlockSpec((tm, D_GRP), lambda i: (jnp.minimum(i, prompt_tiles - 1), 0))
    sample = pl.BlockSpec((tm, D_GRP), lambda i: (jnp.maximum(i - prompt_tiles, 0), 0))
    return pl.pallas_call(
        functools.partial(_out_proj_body, prompt_tiles=prompt_tiles),
        grid=(t // tm,),
        in_specs=[pl.BlockSpec((tm, d), lambda i: (i, 0)), prompt, sample, prompt, sample,
                  pl.BlockSpec((1, D_GRP), lambda i: (0, 0)),
                  pl.BlockSpec(w_bf16.shape, lambda i: (0, 0))],
        out_specs=pl.BlockSpec((tm, d), lambda i: (i, 0)),
        out_shape=jax.ShapeDtypeStruct((t, d), f32),
        compiler_params=_params("parallel"),
        name="out_proj",
    )(x, osb_p, osb_s, orw_p, orw_s, scale.reshape(1, -1), w_bf16)


def _rms_rows(x, g):
    return x * lax.rsqrt(jnp.mean(x * x, axis=-1, keepdims=True) + RMS_EPS) * g


def _ffn_body(x_ref, g_ref, wg_ref, wu_ref, wd_ref, o_ref, h_scr):
    f = pl.program_id(1)

    @pl.when(f == 0)
    def _():
        x = x_ref[...]
        h_scr[...] = _rms_rows(x, g_ref[...]).astype(bf16)
        o_ref[...] = x

    h = h_scr[...]
    act = jax.nn.silu(jnp.dot(h, wg_ref[...], preferred_element_type=f32)) * jnp.dot(h, wu_ref[...], preferred_element_type=f32)
    o_ref[...] += jnp.dot(act.astype(bf16), wd_ref[...], preferred_element_type=f32)


def _ffn(x, g, wg, wu, wd, tm=512, tf=512):
    t, d = x.shape
    dff = wg.shape[1]
    return pl.pallas_call(
        _ffn_body,
        grid=(t // tm, dff // tf),
        in_specs=[pl.BlockSpec((tm, d), lambda i, f: (i, 0)),
                  pl.BlockSpec((1, d), lambda i, f: (0, 0)),
                  pl.BlockSpec((d, tf), lambda i, f: (0, f)),
                  pl.BlockSpec((d, tf), lambda i, f: (0, f)),
                  pl.BlockSpec((tf, d), lambda i, f: (f, 0))],
        out_specs=pl.BlockSpec((tm, d), lambda i, f: (i, 0)),
        out_shape=jax.ShapeDtypeStruct((t, d), f32),
        scratch_shapes=[pltpu.VMEM((tm, d), bf16)],
        compiler_params=_params("parallel", "arbitrary"),
        name="ffn_dense",
    )(x, g.reshape(1, d), wg, wu, wd)


def _top2_gates(logits):
    lane = _iota(logits.shape, 1)
    valid = lane < N_EXPERTS
    m = jnp.max(jnp.where(valid, logits, -jnp.inf), axis=-1, keepdims=True)
    e = jnp.where(valid, jnp.exp(logits - m), 0.0)
    p = e / jnp.sum(e, axis=-1, keepdims=True)
    p1 = jnp.max(p, axis=-1, keepdims=True)
    i1 = jnp.min(jnp.where(valid & (p == p1), lane, LANES), axis=-1, keepdims=True)
    rest = jnp.where(valid & (lane != i1), p, -1.0)
    p2 = jnp.max(rest, axis=-1, keepdims=True)
    i2 = jnp.min(jnp.where(rest == p2, lane, LANES), axis=-1, keepdims=True)
    tot = p1 + p2
    gates = jnp.where(lane == i1, p1 / tot, jnp.where(lane == i2, p2 / tot, 0.0))
    chosen = (lane == i1 + N_EXPERTS) | (lane == i2 + N_EXPERTS)
    return jnp.where(chosen, 1.0, gates)


def _router_body(x_ref, g_ref, router_ref, h_ref, gates_ref):
    hn = _rms_rows(x_ref[...], g_ref[...])
    h_ref[...] = hn
    gates_ref[...] = _top2_gates(_mm(hn, router_ref[...], NN, 3, 3))


def _router(x, g, router, tm=512):
    t, d = x.shape
    router_pad = jnp.zeros((d, LANES), f32).at[:, :router.shape[1]].set(router)
    return pl.pallas_call(
        _router_body,
        grid=(t // tm,),
        in_specs=[pl.BlockSpec((tm, d), lambda i: (i, 0)),
                  pl.BlockSpec((1, d), lambda i: (0, 0)),
                  pl.BlockSpec((d, LANES), lambda i: (0, 0))],
        out_specs=[pl.BlockSpec((tm, d), lambda i: (i, 0)), pl.BlockSpec((tm, LANES), lambda i: (i, 0))],
        out_shape=[jax.ShapeDtypeStruct((t, d), f32), jax.ShapeDtypeStruct((t, LANES), f32)],
        compiler_params=_params("parallel"),
        name="moe_router",
    )(x, g.reshape(1, d), router_pad)


def _for_rows(n, fn):
    def octet(o, c):
        for u in range(SUBLANES):
            fn(o * SUBLANES + u)
        return c
    lax.fori_loop(0, n // SUBLANES, octet, 0)


def _dispatch_body(pos_ref, h_ref, init_hbm, out_hbm, sem, *, tm):
    del init_hbm

    def copy(s, r):
        return pltpu.make_async_copy(h_ref.at[pl.ds(r, 1), :], out_hbm.at[pl.ds(pos_ref[0, 0, s * tm + r], 1), :], sem)

    for s in range(2):
        _for_rows(tm, lambda r: copy(s, r).start())
    for s in range(2):
        _for_rows(tm, lambda r: copy(s, r).wait())


def _expert_rows_body(te_ref, nr_ref, x_ref, wg_ref, wu_ref, wd_ref, o_ref, xb, acc):
    i = pl.program_id(0)
    f = pl.program_id(1)
    used = nr_ref[i] > 0
    last = f == pl.num_programs(1) - 1

    @pl.when(used & (f == 0))
    def _():
        xb[...] = x_ref[...].astype(bf16)
        acc[...] = jnp.zeros_like(acc)

    @pl.when(used)
    def _():
        h = xb[...]
        act = jax.nn.silu(jnp.dot(h, wg_ref[0], preferred_element_type=f32)) * jnp.dot(h, wu_ref[0], preferred_element_type=f32)
        acc[...] += jnp.dot(act.astype(bf16), wd_ref[0], preferred_element_type=f32)

    @pl.when(used & last)
    def _():
        o_ref[...] = acc[...]

    @pl.when(jnp.logical_not(used) & last)
    def _():
        o_ref[...] = jnp.zeros_like(o_ref)


def _combine_body(pos_ref, x_ref, g_ref, y_hbm, o_ref, ybuf, sem, *, tc):
    def copy(s, r):
        return pltpu.make_async_copy(y_hbm.at[pl.ds(pos_ref[0, 0, s * tc + r], 1), :], ybuf.at[s, pl.ds(r, 1), :], sem)

    for s in range(2):
        _for_rows(tc, lambda r: copy(s, r).start())
    for s in range(2):
        _for_rows(tc, lambda r: copy(s, r).wait())
    g = g_ref[...]
    o_ref[...] = x_ref[...] + (ybuf[0] * g[:, 0:1] + ybuf[1] * g[:, 1:2])


def _moe(x, g, router, wg, wu, wd, tm=512, tf=256):
    t, d = x.shape
    ne, _, de = wg.shape
    h, gates = _router(x, g, router)

    sel = (gates[:, ne:2 * ne] > 0.0).astype(jnp.int32)
    cnt = sel.sum(axis=0)
    cnt_pad = (cnt + tm - 1) // tm * tm
    ends = jnp.cumsum(cnt_pad)
    rank = jnp.cumsum(sel, axis=0) - sel
    slot = jnp.cumsum(sel, axis=1) - sel
    n_rows = 2 * t + ne * tm
    n_tiles = n_rows // tm
    row_of = (ends - cnt_pad)[None, :] + rank
    by_slot = lambda v: jnp.stack([jnp.sum(jnp.where((sel > 0) & (slot == s), v, 0), axis=1) for s in range(2)], axis=1)
    pos = by_slot(row_of).astype(jnp.int32)
    slot_gate = jnp.zeros((t, LANES), f32).at[:, :2].set(by_slot(gates[:, :ne]))
    tile_row0 = jnp.arange(n_tiles, dtype=jnp.int32) * tm
    tile_expert = jnp.minimum(jnp.sum(tile_row0[:, None] >= ends[None, :], axis=1), ne - 1).astype(jnp.int32)
    tile_real = jnp.where(tile_row0 < ends[-1], cnt[tile_expert] - (tile_row0 - (ends - cnt_pad)[tile_expert]), 0)
    tile_real = jnp.clip(tile_real, 0, tm).astype(jnp.int32)
    tile_expert = tile_expert[jnp.minimum(jnp.arange(n_tiles), ends[-1] // tm - 1)]

    dma_params = pltpu.CompilerParams(dimension_semantics=("arbitrary",), vmem_limit_bytes=VMEM_LIMIT,
                                      disable_bounds_checks=True)
    pos_rows = lambda n: pl.BlockSpec((1, 1, 2 * n), lambda i: (i, 0, 0), memory_space=pltpu.SMEM)
    by_tile = lambda n: pos.reshape(t // n, n, 2).transpose(0, 2, 1).reshape(t // n, 1, 2 * n)
    xs = pl.pallas_call(
        functools.partial(_dispatch_body, tm=tm),
        grid=(t // tm,),
        in_specs=[pos_rows(tm), pl.BlockSpec((tm, d), lambda i: (i, 0)), pl.BlockSpec(memory_space=pl.ANY)],
        out_specs=pl.BlockSpec(memory_space=pl.ANY),
        out_shape=jax.ShapeDtypeStruct((n_rows, d), f32),
        scratch_shapes=[pltpu.SemaphoreType.DMA(())],
        input_output_aliases={2: 0},
        compiler_params=dma_params,
        name="moe_dispatch",
    )(by_tile(tm), h, jnp.zeros((n_rows, d), f32))

    nf = de // tf
    fblk = lambda i, f, te, nr: jnp.where(nr[i] > 0, f, nf - 1)
    grid_spec = pltpu.PrefetchScalarGridSpec(
        num_scalar_prefetch=2,
        grid=(n_tiles, nf),
        in_specs=[pl.BlockSpec((tm, d), lambda i, f, te, nr: (i, 0)),
                  pl.BlockSpec((1, d, tf), lambda i, f, te, nr: (te[i], 0, fblk(i, f, te, nr))),
                  pl.BlockSpec((1, d, tf), lambda i, f, te, nr: (te[i], 0, fblk(i, f, te, nr))),
                  pl.BlockSpec((1, tf, d), lambda i, f, te, nr: (te[i], fblk(i, f, te, nr), 0))],
        out_specs=pl.BlockSpec((tm, d), lambda i, f, te, nr: (i, 0)),
        scratch_shapes=[pltpu.VMEM((tm, d), bf16), pltpu.VMEM((tm, d), f32)],
    )
    y = pl.pallas_call(
        _expert_rows_body,
        grid_spec=grid_spec,
        out_shape=jax.ShapeDtypeStruct((n_rows, d), f32),
        compiler_params=_params("parallel", "arbitrary"),
        name="moe_experts",
    )(tile_expert, tile_real, xs, wg, wu, wd)

    tc = 256
    return pl.pallas_call(
        functools.partial(_combine_body, tc=tc),
        grid=(t // tc,),
        in_specs=[pos_rows(tc),
                  pl.BlockSpec((tc, d), lambda i: (i, 0)),
                  pl.BlockSpec((tc, LANES), lambda i: (i, 0)),
                  pl.BlockSpec(memory_space=pl.ANY)],
        out_specs=pl.BlockSpec((tc, d), lambda i: (i, 0)),
        out_shape=jax.ShapeDtypeStruct((t, d), f32),
        scratch_shapes=[pltpu.VMEM((2, tc, d), f32), pltpu.SemaphoreType.DMA(())],
        compiler_params=dma_params,
        name="moe_combine",
    )(by_tile(tc), x, slot_gate, y)


def kernel(x_prompt, x_sample, cache_k, cache_v, page_table, state_wkv, state_shift, norm_mix_g, w_in, sb_q_norm_g, sb_k_norm_g, sb_logit_bias, sb_out_scale, rw_shift_mu, rw_w0, rw_w2, rw_a0, rw_a2, rw_g2, rw_k_k, rw_k_a, rw_r_k, rw_ln_g, rw_ln_b, w_out, norm_ffn_g, ffn_w_gate, ffn_w_up, ffn_w_down, moe_router, moe_w_gate, moe_w_up, moe_w_down):
    bp, tp, d = x_prompt.shape
    nb, ts, _ = x_sample.shape
    depth = w_in.shape[0]
    n_p = bp * tp
    x = jnp.concatenate([x_prompt.reshape(n_p, d), x_sample.reshape(nb * ts, d)], axis=0)
    cache_k_t = cache_k.transpose(0, 1, 3, 4, 2)
    cache_v_t = cache_v.transpose(0, 1, 3, 4, 2)
    rw0 = 3 * D_GRP

    kp, vp, ksm, vsm, wp, wsm, sp, ssm = [], [], [], [], [], [], [], []
    for l in range(depth):
        proj = _norm_proj(x, norm_mix_g[l], w_in[l].astype(bf16))
        q, k, r, lw, k2, v_rw, a, b, gate = _mix_prep(
            proj, state_shift[l], n_p, tp, ts, rw_shift_mu[l], sb_q_norm_g[l], sb_k_norm_g[l], rw_w0[l], rw_w2[l],
            rw_a0[l], rw_a2[l], rw_g2[l], rw_k_k[l], rw_k_a[l])
        v = proj[:, 2 * D_GRP:3 * D_GRP]

        o_sb_p = _sb_prompt(q, k, proj, sb_logit_bias[l], bp, tp)
        o_sb_s = _sb_sample(q, k, proj, sb_logit_bias[l], cache_k_t, cache_v_t, page_table, l, n_p, ts)
        o_sb_s = o_sb_s.transpose(0, 2, 1, 3).reshape(nb * ts, D_GRP)

        rw_in = (r, lw, k2, v_rw, a, b, gate)
        o_rw_p, wkv_p = _rwkv_scan(rw_in, rw_r_k[l], rw_ln_g[l], rw_ln_b[l], None, bp, tp, 0, c=64, t_blk=256)
        o_rw_s, wkv_s = _rwkv_scan(rw_in, rw_r_k[l], rw_ln_g[l], rw_ln_b[l], state_wkv[l], nb, ts, n_p, c=ts, t_blk=ts)

        x = _out_proj(x, o_sb_p, o_sb_s, o_rw_p, o_rw_s, sb_out_scale[l], w_out[l].astype(bf16))
        i = l // 2
        if l % 2 == 0:
            x = _ffn(x, norm_ffn_g[l], ffn_w_gate[i].astype(bf16), ffn_w_up[i].astype(bf16), ffn_w_down[i].astype(bf16))
        else:
            x = _moe(x, norm_ffn_g[l], moe_router[i], moe_w_gate[i].astype(bf16), moe_w_up[i].astype(bf16),
                     moe_w_down[i].astype(bf16))

        kp.append(k[:n_p].reshape(bp, tp, N_HEADS, HEAD_DIM))
        vp.append(v[:n_p].reshape(bp, tp, N_HEADS, HEAD_DIM))
        ksm.append(k[n_p:].reshape(nb, ts, N_HEADS, HEAD_DIM))
        vsm.append(v[n_p:].reshape(nb, ts, N_HEADS, HEAD_DIM))
        wp.append(wkv_p)
        wsm.append(wkv_s)
        sp.append(proj[tp - 1:n_p:tp, rw0:])
        ssm.append(proj[n_p + ts - 1::ts, rw0:])

    return (x[:n_p].reshape(bp, tp, d), x[n_p:].reshape(nb, ts, d), jnp.stack(kp), jnp.stack(vp), jnp.stack(ksm),
            jnp.stack(vsm), jnp.stack(wp), jnp.stack(wsm), jnp.stack(sp), jnp.stack(ssm))
```

```python
import functools

import jax
import jax.numpy as jnp
from jax import lax
from jax.experimental import pallas as pl
from jax.experimental.pallas import tpu as pltpu

f32 = jnp.float32
bf16 = jnp.bfloat16

D_MODEL = 2048
HEAD_DIM = 64
N_HEADS = 16
D_GRP = N_HEADS * HEAD_DIM
R_DECAY = 64
R_AAA = 64
R_GATE = 128
N_SHIFT = 3 * D_GRP + R_DECAY + R_AAA + R_GATE
N_IN = 3 * D_GRP + N_SHIFT
N_EXPERTS = 8
PAGE = 128
RMS_EPS = 1e-6
LN_X_EPS = 64e-5
QK_SCALE = HEAD_DIM ** -0.5

LANES = 128
SUBLANES = 8
VMEM_LIMIT = 56 * 1024 * 1024

NN = (((1,), (0,)), ((), ()))
NT = (((1,), (1,)), ((), ()))
TN = (((0,), (0,)), ((), ()))


def _params(*sem):
    return pltpu.CompilerParams(dimension_semantics=sem, vmem_limit_bytes=VMEM_LIMIT)


def _split(x, n):
    out, r = [], x
    for i in range(n):
        h = r.astype(bf16)
        out.append(h)
        if i + 1 < n:
            r = r - h.astype(f32)
    return out


def _mm(a, b, dn=NN, pa=1, pb=1):
    sa = [a] if a.dtype == bf16 else _split(a, pa)
    sb = [b] if b.dtype == bf16 else _split(b, pb)
    order = max(len(sa), len(sb))
    acc = None
    for i, ai in enumerate(sa):
        for j, bj in enumerate(sb):
            if i + j < order:
                t = lax.dot_general(ai, bj, dn, preferred_element_type=f32)
                acc = t if acc is None else acc + t
    return acc


def _softplus(z):
    return jnp.maximum(z, 0.0) + jnp.log1p(jnp.exp(-jnp.abs(z)))


def _iota(shape, dim):
    return lax.broadcasted_iota(jnp.int32, shape, dim)


def _head(x, h):
    return x[:, h * HEAD_DIM:(h + 1) * HEAD_DIM]


def _norm_proj_body(x_ref, g_ref, w_ref, o_ref, h_scr):
    @pl.when(pl.program_id(1) == 0)
    def _():
        x = x_ref[...]
        y = x * lax.rsqrt(jnp.mean(x * x, axis=-1, keepdims=True) + RMS_EPS)
        h_scr[...] = (y * g_ref[...]).astype(bf16)

    o_ref[...] = jnp.dot(h_scr[...], w_ref[...], preferred_element_type=f32)


def _norm_proj(x, g, w_bf16, tm=512, tn=1280):
    t, d = x.shape
    n = w_bf16.shape[1]
    return pl.pallas_call(
        _norm_proj_body,
        grid=(t // tm, n // tn),
        in_specs=[pl.BlockSpec((tm, d), lambda i, j: (i, 0)),
                  pl.BlockSpec((1, d), lambda i, j: (0, 0)),
                  pl.BlockSpec((d, tn), lambda i, j: (0, j))],
        out_specs=pl.BlockSpec((tm, tn), lambda i, j: (i, j)),
        out_shape=jax.ShapeDtypeStruct((t, n), f32),
        scratch_shapes=[pltpu.VMEM((tm, d), bf16)],
        compiler_params=_params("parallel", "arbitrary"),
        name="norm_proj",
    )(x, g.reshape(1, d), w_bf16)


def _mix_prep_body(proj_ref, before_ref, state_ref, mu_ref, gq_ref, gk_ref, w0_ref, w2_ref, a0_ref, a2_ref, g2_ref,
                   kk_ref, ka_ref, r1_ref, r2_ref,
                   q_ref, k_ref, r_ref, lw_ref, k2_ref, v_ref, a_ref, b_ref, g_ref,
                   *, prompt_tiles, prompt_len, sample_len):
    tm = proj_ref.shape[0]
    i = pl.program_id(0)
    seg_ones = r1_ref[...]
    seg_bcast = r2_ref[...]
    row = _iota((tm, 1), 0)
    is_sample = i >= prompt_tiles
    keep_first = jnp.where((i * tm) % prompt_len == 0, 0.0, 1.0)
    n_seq = tm // sample_len
    pick = ((_iota((tm, n_seq), 0) == _iota((tm, n_seq), 1) * sample_len)).astype(bf16)
    opens_sample = row % sample_len == 0

    def head_sum(x):
        return _mm(x, seg_ones, NN, 2, 1)

    def head_spread(s):
        return _mm(s, seg_bcast, NN, 2, 1)

    q = proj_ref[:, 0:D_GRP]
    qn = q * head_spread(lax.rsqrt(head_sum(q * q) * (1.0 / HEAD_DIM) + RMS_EPS))
    q_ref[...] = qn * gq_ref[...] * QK_SCALE
    k = proj_ref[:, D_GRP:2 * D_GRP]
    kn = k * head_spread(lax.rsqrt(head_sum(k * k) * (1.0 / HEAD_DIM) + RMS_EPS))
    k_ref[...] = kn * gk_ref[...]

    base = 3 * D_GRP

    def shifted(lo, hi):
        p = proj_ref[:, base + lo:base + hi]
        before = jnp.where(row == 0, before_ref[SUBLANES - 1:SUBLANES, base + lo:base + hi], pltpu.roll(p, 1, axis=0))
        in_prompt = jnp.where(row == 0, before * keep_first, before)
        in_sample = jnp.where(opens_sample, _mm(pick, state_ref[:, lo:hi], NN, 1, 3), before)
        prev = jnp.where(is_sample, in_sample, in_prompt)
        return p + (prev - p) * mu_ref[:, lo:hi]

    r = shifted(0, D_GRP)
    kr = shifted(D_GRP, 2 * D_GRP)
    v = shifted(2 * D_GRP, 3 * D_GRP)
    tail = shifted(3 * D_GRP, N_SHIFT)
    wd = tail[:, 0:R_DECAY]
    ad = tail[:, R_DECAY:R_DECAY + R_AAA]
    gd = tail[:, R_DECAY + R_AAA:]

    w_pre = w0_ref[...] + _mm(jnp.tanh(wd), w2_ref[...])
    w_log = -_softplus(-w_pre) - 0.5
    lw_ref[...] = -jnp.exp(w_log)
    a_lr = jax.nn.sigmoid(a0_ref[...] + _mm(ad, a2_ref[...]))
    g_ref[...] = _mm(jax.nn.sigmoid(gd), g2_ref[...])

    kk = kr * kk_ref[...]
    kk = kk * head_spread(lax.rsqrt(jnp.maximum(head_sum(kk * kk), 1e-24)))
    r_ref[...] = r
    k2_ref[...] = kr * (1.0 + (a_lr - 1.0) * ka_ref[...])
    v_ref[...] = v
    a_ref[...] = -kk
    b_ref[...] = kk * a_lr


def _mix_prep(proj, shift_state, n_prompt, prompt_len, sample_len, mu, gq, gk, w0, w2, a0, a2, g2, k_k, k_a, tm=128):
    t = proj.shape[0]
    heads = _iota((D_GRP, LANES), 0) // HEAD_DIM
    seg_ones = (heads == _iota((D_GRP, LANES), 1)).astype(bf16)
    seg_bcast = seg_ones.T
    prompt_tiles = n_prompt // tm
    n_seq = tm // sample_len
    last_state_blk = shift_state.shape[0] // n_seq - 1
    row = lambda n: pl.BlockSpec((1, n), lambda i: (0, 0))
    full = lambda a: pl.BlockSpec(a.shape, lambda i: (0, 0))
    tok = lambda n: pl.BlockSpec((tm, n), lambda i: (i, 0))
    before = pl.BlockSpec((SUBLANES, N_IN), lambda i: (jnp.maximum(i * (tm // SUBLANES) - 1, 0), 0))
    state = pl.BlockSpec((n_seq, N_SHIFT), lambda i: (jnp.clip(i - prompt_tiles, 0, last_state_blk), 0))
    outs = pl.pallas_call(
        functools.partial(_mix_prep_body, prompt_tiles=prompt_tiles, prompt_len=prompt_len, sample_len=sample_len),
        grid=(t // tm,),
        in_specs=[tok(N_IN), before, state, row(N_SHIFT), row(D_GRP), row(D_GRP), row(D_GRP), full(w2), row(D_GRP),
                  full(a2), full(g2), row(D_GRP), row(D_GRP), full(seg_ones), full(seg_bcast)],
        out_specs=[tok(D_GRP)] * 9,
        out_shape=[jax.ShapeDtypeStruct((t, D_GRP), f32)] * 9,
        compiler_params=_params("parallel"),
        name="mix_prep",
    )(proj, proj, shift_state, mu.reshape(1, -1), jnp.tile(gq, N_HEADS).reshape(1, -1), jnp.tile(gk, N_HEADS).reshape(1, -1),
      w0.reshape(1, -1), w2.astype(bf16), a0.reshape(1, -1), a2.astype(bf16), g2.astype(bf16),
      k_k.reshape(1, -1), k_a.reshape(1, -1), seg_ones, seg_bcast)
    return outs


def _suffix_matrix(tk):
    r = _iota((2 * tk, tk), 0) % tk
    c = _iota((2 * tk, tk), 1)
    return jnp.where(r > c, 1.0, 0.0).astype(bf16)


def _sb_scores(z, suffix_mat, mask):
    sp = jnp.maximum(z, 0.0) + jnp.log(1.0 + jnp.exp(-jnp.abs(z)))
    log_1m = -sp
    if mask is not None:
        log_1m = jnp.where(mask, log_1m, 0.0)
    hi = log_1m.astype(bf16)
    lo = (log_1m - hi.astype(f32)).astype(bf16)
    later = jnp.dot(jnp.concatenate([hi, lo], axis=1), suffix_mat, preferred_element_type=f32)
    return (z - sp) + later, jnp.sum(log_1m, axis=-1, keepdims=True)


def _sb_prompt_body(q_ref, k_ref, v_ref, bias_ref, sfx_ref, o_ref, acc_scr, *carry_scr, tq, tk, hp):
    i = pl.program_id(2)
    ratio = tq // tk
    hs = range(hp)
    lane_head = _iota((tq, hp * HEAD_DIM), 1) // HEAD_DIM
    qall = q_ref[...]
    qh = [jnp.where(lane_head == h, qall, 0.0).astype(bf16) for h in hs]
    bias = [bias_ref[0, h:h + 1, :] for h in hs]
    sfx = sfx_ref[...]
    for h in hs:
        carry_scr[h][...] = jnp.zeros((tq, 1), f32)
    acc_scr[...] = jnp.zeros_like(acc_scr)

    def block(j, masked, lo=0):
        n = tq - lo
        ks = pl.ds(pl.multiple_of(j * tk, tk), tk)
        kb = k_ref[ks, :].astype(bf16)
        vb = v_ref[ks, :].astype(bf16)
        mask = (j * tk + _iota((n, tk), 1) < i * tq + lo + _iota((n, tk), 0)) if masked else None
        zs = [_mm(qh[h][lo:], kb, NT) + bias[h] for h in hs]
        sc = [_sb_scores(z, sfx, mask) for z in zs]
        ws = []
        for h in hs:
            carry = carry_scr[h][lo:, :]
            a = jnp.exp(sc[h][0] + carry)
            if masked:
                a = jnp.where(mask, a, 0.0)
            carry_scr[h][lo:, :] = carry + sc[h][1]
            ws.append(a.astype(bf16))
        os_ = [_mm(ws[h], vb[:, (h // 2) * LANES:(h // 2 + 1) * LANES]) for h in hs]
        first = _iota((n, LANES), 1) < HEAD_DIM
        upd = [jnp.where(first, os_[2 * p], os_[2 * p + 1]) for p in range(hp // 2)]
        acc_scr[lo:, :] += jnp.concatenate(upd, axis=1)

    for c in reversed(range(ratio)):
        block(i * ratio + c, True, lo=c * tk)

    def step(jj, c):
        block(i * ratio - 1 - jj, False)
        return c

    lax.fori_loop(0, i * ratio, step, 0)
    o_ref[...] = acc_scr[...]


def _sb_prompt(q, k, proj, bias, nb, t, tq=512, tk=PAGE, hp=4):
    groups = N_HEADS // hp
    width = hp * HEAD_DIM
    nq = t // tq
    bias_rows = jnp.broadcast_to(bias.astype(f32).reshape(groups, hp, 1), (groups, hp, tk))
    v_blk0 = 2 * D_GRP // width
    return pl.pallas_call(
        functools.partial(_sb_prompt_body, tq=tq, tk=tk, hp=hp),
        grid=(nb, groups, nq),
        in_specs=[pl.BlockSpec((tq, width), lambda b, p, i: (b * nq + i, p)),
                  pl.BlockSpec((t, width), lambda b, p, i: (b, p)),
                  pl.BlockSpec((t, width), lambda b, p, i: (b, v_blk0 + p)),
                  pl.BlockSpec((1, hp, tk), lambda b, p, i: (p, 0, 0)),
                  pl.BlockSpec((2 * tk, tk), lambda b, p, i: (0, 0))],
        out_specs=pl.BlockSpec((tq, width), lambda b, p, i: (b * nq + i, p)),
        out_shape=jax.ShapeDtypeStruct((nb * t, D_GRP), f32),
        scratch_shapes=[pltpu.VMEM((tq, width), f32)] + [pltpu.VMEM((tq, 1), f32)] * hp,
        compiler_params=_params("parallel", "parallel", "arbitrary"),
        name="sb_prompt",
    )(q, k, proj, bias_rows, _suffix_matrix(tk))


def _sb_sample_body(pt_ref, q_ref, kn_ref, vn_ref, bias_ref, sfx_ref, *refs, n_new, pp):
    kc_refs, vc_refs = refs[:pp], refs[pp:2 * pp]
    o_ref, q_scr, acc_scr, carry_scr = refs[2 * pp:]
    s = pl.program_id(1)
    rows = N_HEADS * n_new
    heads = range(N_HEADS)
    sfx = sfx_ref[...]
    bias = bias_ref[...]
    hrows = lambda x, h: x[h * n_new:(h + 1) * n_new]

    @pl.when(s == 0)
    def _():
        q = q_ref[...]
        for h in heads:
            q_scr[h] = _head(q, h)
        pad = jnp.zeros((PAGE - n_new, HEAD_DIM), f32)
        kn, vn = kn_ref[...], vn_ref[...]
        zs = [_mm(_head(q, h).astype(bf16), jnp.concatenate([_head(kn, h), pad], axis=0).astype(bf16), NT)
              for h in heads]
        causal = _iota((rows, PAGE), 1) < _iota((rows, PAGE), 0) % n_new
        lbs, tot = _sb_scores(jnp.concatenate(zs, axis=0) + bias, sfx, causal)
        a = jnp.where(causal, jnp.exp(lbs), 0.0)
        os_ = [_mm(hrows(a, h).astype(bf16), jnp.concatenate([_head(vn, h), pad], axis=0).astype(bf16))
               for h in heads]
        acc_scr[...] = jnp.concatenate(os_, axis=0)
        carry_scr[...] = tot

    qb = [q_scr[h].astype(bf16) for h in heads]
    zs = [jnp.concatenate([_mm(qb[h], kc_refs[u][0, 0, h].astype(bf16)) for h in heads], axis=0) + bias
          for u in range(pp)]
    sc = [_sb_scores(z, sfx, None) for z in zs]
    carry = carry_scr[...]
    ws = []
    for u in range(pp):
        ws.append(jnp.exp(sc[u][0] + carry))
        carry = carry + sc[u][1]
    carry_scr[...] = carry
    os_ = [[_mm(hrows(ws[u], h).astype(bf16), vc_refs[u][0, 0, h].astype(bf16), NT) for h in heads]
           for u in range(pp)]
    upd = [functools.reduce(lambda x, y: x + y, [os_[u][h] for u in range(pp)]) for h in heads]
    acc_scr[...] += jnp.concatenate(upd, axis=0)

    @pl.when(s == pl.num_programs(1) - 1)
    def _():
        acc = acc_scr[...]
        for h in heads:
            o_ref[0, h] = hrows(acc, h)


def _sb_sample(q, k, proj, bias, cache_k_t, cache_v_t, page_table, layer, row0, n_new, pp=8):
    nb, n_pages = page_table.shape
    rows = N_HEADS * n_new
    bias_rows = jnp.repeat(bias.astype(f32), n_new).reshape(rows, 1)
    blk0 = row0 // n_new
    v_blk0 = 2 * D_GRP // D_GRP

    def page(u):
        return lambda b, s, pt: (layer, pt[b, n_pages - 1 - (s * pp + u)], 0, 0, 0)

    tok = lambda cb: pl.BlockSpec((n_new, D_GRP), lambda b, s, pt: (blk0 + b, cb))
    page_specs = [pl.BlockSpec((1, 1, N_HEADS, HEAD_DIM, PAGE), page(u)) for u in range(pp)]
    grid_spec = pltpu.PrefetchScalarGridSpec(
        num_scalar_prefetch=1,
        grid=(nb, n_pages // pp),
        in_specs=[tok(0), tok(0), tok(v_blk0),
                  pl.BlockSpec((rows, 1), lambda b, s, pt: (0, 0)),
                  pl.BlockSpec((2 * PAGE, PAGE), lambda b, s, pt: (0, 0))] + page_specs + page_specs,
        out_specs=pl.BlockSpec((1, N_HEADS, n_new, HEAD_DIM), lambda b, s, pt: (b, 0, 0, 0)),
        scratch_shapes=[pltpu.VMEM((N_HEADS, n_new, HEAD_DIM), f32), pltpu.VMEM((rows, HEAD_DIM), f32),
                        pltpu.VMEM((rows, 1), f32)],
    )
    return pl.pallas_call(
        functools.partial(_sb_sample_body, n_new=n_new, pp=pp),
        grid_spec=grid_spec,
        out_shape=jax.ShapeDtypeStruct((nb, N_HEADS, n_new, HEAD_DIM), f32),
        compiler_params=_params("parallel", "arbitrary"),
        name="sb_sample",
    )(page_table, q, k, proj, bias_rows, _suffix_matrix(PAGE),
      *([cache_k_t] * pp), *([cache_v_t] * pp))


def _unit_lower_inverses(a_strict, c):
    r = _iota((c, c), 0)
    col = _iota((c, c), 1)
    mm = lambda x, y: _mm(x, y, NN, 2, 2)
    base = SUBLANES
    ps = [jnp.where(r // base == col // base, a, 0.0) for a in a_strict]
    eye = jnp.where(r == col, 1.0, 0.0)
    ts = [eye + p for p in ps]
    for _ in range(2):
        ps = [mm(p, p) for p in ps]
        ts = [t + mm(t, p) for t, p in zip(ts, ps)]
    s = base
    while s < c:
        join = (r // (2 * s) == col // (2 * s)) & (r // s != col // s)
        low = [mm(jnp.where(join, a, 0.0), t) for a, t in zip(a_strict, ts)]
        ts = [t + mm(t, x) for t, x in zip(ts, low)]
        s *= 2
    return ts


def _rwkv_body(*refs, c, n_chunks, has_init):
    if has_init:
        (r_ref, lw_ref, k_ref, v_ref, a_ref, b_ref, gate_ref, rk_ref, lng_ref, lnb_ref, s0_ref,
         o_ref, sout_ref, s_scr) = refs
    else:
        (r_ref, lw_ref, k_ref, v_ref, a_ref, b_ref, gate_ref, rk_ref, lng_ref, lnb_ref,
         o_ref, sout_ref, s_scr) = refs
    tstep = pl.program_id(1)
    heads = range(N_HEADS)

    @pl.when(tstep == 0)
    def _():
        s_scr[...] = s0_ref[0] if has_init else jnp.zeros_like(s_scr)

    row = _iota((c, c), 0)
    col = _iota((c, c), 1)
    incl = jnp.where(row >= col, 1.0, 0.0).astype(bf16)
    row2 = _iota((2 * c, 2 * c), 0)
    col2 = _iota((2 * c, 2 * c), 1)
    lower = (col2 % c) < jnp.where(row2 < c, row2, row2 - c + 1)
    mm = lambda x, y, dn=NN: _mm(x, y, dn, 2, 2)
    mm1 = lambda x, y, dn=NN: _mm(x, y, dn, 1, 1)
    stack = lambda x, y: jnp.concatenate([x, y], axis=0)

    def chunk(ci, carry):
        ts = pl.ds(pl.multiple_of(ci * c, c), c)
        r, lw, k, v = r_ref[ts, :], lw_ref[ts, :], k_ref[ts, :], v_ref[ts, :]
        cum = _mm(incl, lw, NN, 1, 3)
        grow = jnp.exp(-cum)
        a_t = a_ref[ts, :] * jnp.exp(cum - lw)
        r_t = r * jnp.exp(cum)
        b_t = b_ref[ts, :] * grow
        k_t = k * grow
        end = jnp.exp(cum[c - 1:c, :])
        left = [stack(_head(a_t, h), _head(r_t, h)) for h in heads]
        right = [stack(_head(b_t, h), _head(k_t, h)) for h in heads]
        vs = [_head(v, h) for h in heads]
        s0 = [s_scr[h] for h in heads]
        gram = [jnp.where(lower, mm(l, rt, NT), 0.0) for l, rt in zip(left, right)]
        from_state = [mm1(l, s, NT) for l, s in zip(left, s0)]
        t_inv = _unit_lower_inverses([g[:c, :c] for g in gram], c)
        akv = [mm1(g[:c, c:], x) for g, x in zip(gram, vs)]
        us = [mm(t, f[:c] + x) for t, f, x in zip(t_inv, from_state, akv)]
        uv = [stack(u, x) for u, x in zip(us, vs)]
        ys = [f[c:] + mm1(g[c:], x) for f, g, x in zip(from_state, gram, uv)]
        s_new = [(s + mm1(x, rt, TN)) * _head(end, h) for h, (s, x, rt) in enumerate(zip(s0, uv, right))]
        for h in heads:
            s_scr[h] = s_new[h]
            y = ys[h]
            mu = jnp.mean(y, axis=-1, keepdims=True)
            yc = y - mu
            var = jnp.mean(yc * yc, axis=-1, keepdims=True)
            yn = yc * lax.rsqrt(var + LN_X_EPS) * _head(lng_ref, h) + _head(lnb_ref, h)
            bonus = jnp.sum(_head(r, h) * _head(k, h) * _head(rk_ref, h), axis=-1, keepdims=True) * vs[h]
            o_ref[ts, h * HEAD_DIM:(h + 1) * HEAD_DIM] = (yn + bonus) * gate_ref[ts, h * HEAD_DIM:(h + 1) * HEAD_DIM]
        return carry

    lax.fori_loop(0, n_chunks, chunk, 0)

    @pl.when(tstep == pl.num_programs(1) - 1)
    def _():
        sout_ref[0] = s_scr[...]


def _rwkv_scan(rw_in, r_k, ln_g, ln_b, s0, nb, t, row0, c, t_blk):
    nt = t // t_blk
    blk0 = row0 // t_blk
    seq = pl.BlockSpec((t_blk, D_GRP), lambda b, ti: (blk0 + b * nt + ti, 0))
    per_lane = pl.BlockSpec((1, D_GRP), lambda b, ti: (0, 0))
    state = pl.BlockSpec((1, N_HEADS, HEAD_DIM, HEAD_DIM), lambda b, ti: (b, 0, 0, 0))
    has_init = s0 is not None
    args = list(rw_in) + [r_k.reshape(1, D_GRP), ln_g.reshape(1, D_GRP), ln_b.reshape(1, D_GRP)]
    in_specs = [seq] * 7 + [per_lane] * 3
    if has_init:
        args.append(s0)
        in_specs.append(state)
    return pl.pallas_call(
        functools.partial(_rwkv_body, c=c, n_chunks=t_blk // c, has_init=has_init),
        grid=(nb, nt),
        in_specs=in_specs,
        out_specs=[pl.BlockSpec((t_blk, D_GRP), lambda b, ti: (b * nt + ti, 0)), state],
        out_shape=[jax.ShapeDtypeStruct((nb * t, D_GRP), f32),
                   jax.ShapeDtypeStruct((nb, N_HEADS, HEAD_DIM, HEAD_DIM), f32)],
        scratch_shapes=[pltpu.VMEM((N_HEADS, HEAD_DIM, HEAD_DIM), f32)],
        compiler_params=_params("parallel", "arbitrary"),
        name="rwkv_scan",
    )(*args)


def _out_proj_body(x_ref, osb_p_ref, osb_s_ref, orw_p_ref, orw_s_ref, scale_ref, w_ref, o_ref, *, prompt_tiles):
    sample = pl.program_id(0) >= prompt_tiles
    o_sb = jnp.where(sample, osb_s_ref[...], osb_p_ref[...])
    o_rw = jnp.where(sample, orw_s_ref[...], orw_p_ref[...])
    acc = jnp.dot((o_sb * scale_ref[...]).astype(bf16), w_ref[0:D_GRP, :], preferred_element_type=f32)
    acc = acc + jnp.dot(o_rw.astype(bf16), w_ref[D_GRP:, :], preferred_element_type=f32)
    o_ref[...] = x_ref[...] + acc


def _out_proj(x, osb_p, osb_s, orw_p, orw_s, scale, w_bf16, tm=512):
    t, d = x.shape
    prompt_tiles = osb_p.shape[0] // tm
    prompt = pl.BlockSpec((tm, D_GRP), lambda i: (jnp.minimum(i, prompt_tiles - 1), 0))
    sample = pl.BlockSpec((tm, D_GRP), lambda i: (jnp.maximum(i - prompt_tiles, 0), 0))
    return pl.pallas_call(
        functools.partial(_out_proj_body, prompt_tiles=prompt_tiles),
        grid=(t // tm,),
        in_specs=[pl.BlockSpec((tm, d), lambda i: (i, 0)), prompt, sample, prompt, sample,
                  pl.BlockSpec((1, D_GRP), lambda i: (0, 0)),
                  pl.BlockSpec(w_bf16.shape, lambda i: (0, 0))],
        out_specs=pl.BlockSpec((tm, d), lambda i: (i, 0)),
        out_shape=jax.ShapeDtypeStruct((t, d), f32),
        compiler_params=_params("parallel"),
        name="out_proj",
    )(x, osb_p, osb_s, orw_p, orw_s, scale.reshape(1, -1), w_bf16)


def _rms_rows(x, g):
    return x * lax.rsqrt(jnp.mean(x * x, axis=-1, keepdims=True) + RMS_EPS) * g


def _ffn_body(x_ref, g_ref, wg_ref, wu_ref, wd_ref, o_ref, h_scr):
    f = pl.program_id(1)

    @pl.when(f == 0)
    def _():
        x = x_ref[...]
        h_scr[...] = _rms_rows(x, g_ref[...]).astype(bf16)
        o_ref[...] = x

    h = h_scr[...]
    act = jax.nn.silu(jnp.dot(h, wg_ref[...], preferred_element_type=f32)) * jnp.dot(h, wu_ref[...], preferred_element_type=f32)
    o_ref[...] += jnp.dot(act.astype(bf16), wd_ref[...], preferred_element_type=f32)


def _ffn(x, g, wg, wu, wd, tm=512, tf=512):
    t, d = x.shape
    dff = wg.shape[1]
    return pl.pallas_call(
        _ffn_body,
        grid=(t // tm, dff // tf),
        in_specs=[pl.BlockSpec((tm, d), lambda i, f: (i, 0)),
                  pl.BlockSpec((1, d), lambda i, f: (0, 0)),
                  pl.BlockSpec((d, tf), lambda i, f: (0, f)),
                  pl.BlockSpec((d, tf), lambda i, f: (0, f)),
                  pl.BlockSpec((tf, d), lambda i, f: (f, 0))],
        out_specs=pl.BlockSpec((tm, d), lambda i, f: (i, 0)),
        out_shape=jax.ShapeDtypeStruct((t, d), f32),
        scratch_shapes=[pltpu.VMEM((tm, d), bf16)],
        compiler_params=_params("parallel", "arbitrary"),
        name="ffn_dense",
    )(x, g.reshape(1, d), wg, wu, wd)


def _top2_gates(logits):
    lane = _iota(logits.shape, 1)
    valid = lane < N_EXPERTS
    m = jnp.max(jnp.where(valid, logits, -jnp.inf), axis=-1, keepdims=True)
    e = jnp.where(valid, jnp.exp(logits - m), 0.0)
    p = e / jnp.sum(e, axis=-1, keepdims=True)
    p1 = jnp.max(p, axis=-1, keepdims=True)
    i1 = jnp.min(jnp.where(valid & (p == p1), lane, LANES), axis=-1, keepdims=True)
    rest = jnp.where(valid & (lane != i1), p, -1.0)
    p2 = jnp.max(rest, axis=-1, keepdims=True)
    i2 = jnp.min(jnp.where(rest == p2, lane, LANES), axis=-1, keepdims=True)
    tot = p1 + p2
    gates = jnp.where(lane == i1, p1 / tot, jnp.where(lane == i2, p2 / tot, 0.0))
    chosen = (lane == i1 + N_EXPERTS) | (lane == i2 + N_EXPERTS)
    return jnp.where(chosen, 1.0, gates)


def _router_body(x_ref, g_ref, router_ref, h_ref, gates_ref):
    hn = _rms_rows(x_ref[...], g_ref[...])
    h_ref[...] = hn
    gates_ref[...] = _top2_gates(_mm(hn, router_ref[...], NN, 3, 3))


def _router(x, g, router, tm=512):
    t, d = x.shape
    router_pad = jnp.zeros((d, LANES), f32).at[:, :router.shape[1]].set(router)
    return pl.pallas_call(
        _router_body,
        grid=(t // tm,),
        in_specs=[pl.BlockSpec((tm, d), lambda i: (i, 0)),
                  pl.BlockSpec((1, d), lambda i: (0, 0)),
                  pl.BlockSpec((d, LANES), lambda i: (0, 0))],
        out_specs=[pl.BlockSpec((tm, d), lambda i: (i, 0)), pl.BlockSpec((tm, LANES), lambda i: (i, 0))],
        out_shape=[jax.ShapeDtypeStruct((t, d), f32), jax.ShapeDtypeStruct((t, LANES), f32)],
        compiler_params=_params("parallel"),
        name="moe_router",
    )(x, g.reshape(1, d), router_pad)


def _for_rows(n, fn):
    def octet(o, c):
        for u in range(SUBLANES):
            fn(o * SUBLANES + u)
        return c
    lax.fori_loop(0, n // SUBLANES, octet, 0)


def _dispatch_body(pos_ref, h_ref, init_hbm, out_hbm, sem, *, tm):
    del init_hbm

    def copy(s, r):
        return pltpu.make_async_copy(h_ref.at[pl.ds(r, 1), :], out_hbm.at[pl.ds(pos_ref[0, 0, s * tm + r], 1), :], sem)

    for s in range(2):
        _for_rows(tm, lambda r: copy(s, r).start())
    for s in range(2):
        _for_rows(tm, lambda r: copy(s, r).wait())


def _expert_rows_body(te_ref, nr_ref, x_ref, wg_ref, wu_ref, wd_ref, o_ref, xb, acc):
    i = pl.program_id(0)
    f = pl.program_id(1)
    used = nr_ref[i] > 0
    last = f == pl.num_programs(1) - 1

    @pl.when(used & (f == 0))
    def _():
        xb[...] = x_ref[...].astype(bf16)
        acc[...] = jnp.zeros_like(acc)

    @pl.when(used)
    def _():
        h = xb[...]
        act = jax.nn.silu(jnp.dot(h, wg_ref[0], preferred_element_type=f32)) * jnp.dot(h, wu_ref[0], preferred_element_type=f32)
        acc[...] += jnp.dot(act.astype(bf16), wd_ref[0], preferred_element_type=f32)

    @pl.when(used & last)
    def _():
        o_ref[...] = acc[...]

    @pl.when(jnp.logical_not(used) & last)
    def _():
        o_ref[...] = jnp.zeros_like(o_ref)


def _combine_body(pos_ref, x_ref, g_ref, y_hbm, o_ref, ybuf, sem, *, tc):
    def copy(s, r):
        return pltpu.make_async_copy(y_hbm.at[pl.ds(pos_ref[0, 0, s * tc + r], 1), :], ybuf.at[s, pl.ds(r, 1), :], sem)

    for s in range(2):
        _for_rows(tc, lambda r: copy(s, r).start())
    for s in range(2):
        _for_rows(tc, lambda r: copy(s, r).wait())
    g = g_ref[...]
    o_ref[...] = x_ref[...] + (ybuf[0] * g[:, 0:1] + ybuf[1] * g[:, 1:2])


def _moe(x, g, router, wg, wu, wd, tm=512, tf=256):
    t, d = x.shape
    ne, _, de = wg.shape
    h, gates = _router(x, g, router)

    sel = (gates[:, ne:2 * ne] > 0.0).astype(jnp.int32)
    cnt = sel.sum(axis=0)
    cnt_pad = (cnt + tm - 1) // tm * tm
    ends = jnp.cumsum(cnt_pad)
    rank = jnp.cumsum(sel, axis=0) - sel
    slot = jnp.cumsum(sel, axis=1) - sel
    n_rows = 2 * t + ne * tm
    n_tiles = n_rows // tm
    row_of = (ends - cnt_pad)[None, :] + rank
    by_slot = lambda v: jnp.stack([jnp.sum(jnp.where((sel > 0) & (slot == s), v, 0), axis=1) for s in range(2)], axis=1)
    pos = by_slot(row_of).astype(jnp.int32)
    slot_gate = jnp.zeros((t, LANES), f32).at[:, :2].set(by_slot(gates[:, :ne]))
    tile_row0 = jnp.arange(n_tiles, dtype=jnp.int32) * tm
    tile_expert = jnp.minimum(jnp.sum(tile_row0[:, None] >= ends[None, :], axis=1), ne - 1).astype(jnp.int32)
    tile_real = jnp.where(tile_row0 < ends[-1], cnt[tile_expert] - (tile_row0 - (ends - cnt_pad)[tile_expert]), 0)
    tile_real = jnp.clip(tile_real, 0, tm).astype(jnp.int32)
    tile_expert = tile_expert[jnp.minimum(jnp.arange(n_tiles), ends[-1] // tm - 1)]

    dma_params = pltpu.CompilerParams(dimension_semantics=("arbitrary",), vmem_limit_bytes=VMEM_LIMIT,
                                      disable_bounds_checks=True)
    pos_rows = lambda n: pl.BlockSpec((1, 1, 2 * n), lambda i: (i, 0, 0), memory_space=pltpu.SMEM)
    by_tile = lambda n: pos.reshape(t // n, n, 2).transpose(0, 2, 1).reshape(t // n, 1, 2 * n)
    xs = pl.pallas_call(
        functools.partial(_dispatch_body, tm=tm),
        grid=(t // tm,),
        in_specs=[pos_rows(tm), pl.BlockSpec((tm, d), lambda i: (i, 0)), pl.BlockSpec(memory_space=pl.ANY)],
        out_specs=pl.BlockSpec(memory_space=pl.ANY),
        out_shape=jax.ShapeDtypeStruct((n_rows, d), f32),
        scratch_shapes=[pltpu.SemaphoreType.DMA(())],
        input_output_aliases={2: 0},
        compiler_params=dma_params,
        name="moe_dispatch",
    )(by_tile(tm), h, jnp.zeros((n_rows, d), f32))

    nf = de // tf
    fblk = lambda i, f, te, nr: jnp.where(nr[i] > 0, f, nf - 1)
    grid_spec = pltpu.PrefetchScalarGridSpec(
        num_scalar_prefetch=2,
        grid=(n_tiles, nf),
        in_specs=[pl.BlockSpec((tm, d), lambda i, f, te, nr: (i, 0)),
                  pl.BlockSpec((1, d, tf), lambda i, f, te, nr: (te[i], 0, fblk(i, f, te, nr))),
                  pl.BlockSpec((1, d, tf), lambda i, f, te, nr: (te[i], 0, fblk(i, f, te, nr))),
                  pl.BlockSpec((1, tf, d), lambda i, f, te, nr: (te[i], fblk(i, f, te, nr), 0))],
        out_specs=pl.BlockSpec((tm, d), lambda i, f, te, nr: (i, 0)),
        scratch_shapes=[pltpu.VMEM((tm, d), bf16), pltpu.VMEM((tm, d), f32)],
    )
    y = pl.pallas_call(
        _expert_rows_body,
        grid_spec=grid_spec,
        out_shape=jax.ShapeDtypeStruct((n_rows, d), f32),
        compiler_params=_params("parallel", "arbitrary"),
        name="moe_experts",
    )(tile_expert, tile_real, xs, wg, wu, wd)

    tc = 256
    return pl.pallas_call(
        functools.partial(_combine_body, tc=tc),
        grid=(t // tc,),
        in_specs=[pos_rows(tc),
                  pl.BlockSpec((tc, d), lambda i: (i, 0)),
                  pl.BlockSpec((tc, LANES), lambda i: (i, 0)),
                  pl.BlockSpec(memory_space=pl.ANY)],
        out_specs=pl.BlockSpec((tc, d), lambda i: (i, 0)),
        out_shape=jax.ShapeDtypeStruct((t, d), f32),
        scratch_shapes=[pltpu.VMEM((2, tc, d), f32), pltpu.SemaphoreType.DMA(())],
        compiler_params=dma_params,
        name="moe_combine",
    )(by_tile(tc), x, slot_gate, y)


def kernel(x_prompt, x_sample, cache_k, cache_v, page_table, state_wkv, state_shift, norm_mix_g, w_in, sb_q_norm_g, sb_k_norm_g, sb_logit_bias, sb_out_scale, rw_shift_mu, rw_w0, rw_w2, rw_a0, rw_a2, rw_g2, rw_k_k, rw_k_a, rw_r_k, rw_ln_g, rw_ln_b, w_out, norm_ffn_g, ffn_w_gate, ffn_w_up, ffn_w_down, moe_router, moe_w_gate, moe_w_up, moe_w_down):
    bp, tp, d = x_prompt.shape
    nb, ts, _ = x_sample.shape
    depth = w_in.shape[0]
    n_p = bp * tp
    x = jnp.concatenate([x_prompt.reshape(n_p, d), x_sample.reshape(nb * ts, d)], axis=0)
    cache_k_t = cache_k.transpose(0, 1, 3, 4, 2)
    cache_v_t = cache_v.transpose(0, 1, 3, 4, 2)
    rw0 = 3 * D_GRP

    kp, vp, ksm, vsm, wp, wsm, sp, ssm = [], [], [], [], [], [], [], []
    for l in range(depth):
        proj = _norm_proj(x, norm_mix_g[l], w_in[l].astype(bf16))
        q, k, r, lw, k2, v_rw, a, b, gate = _mix_prep(
            proj, state_shift[l], n_p, tp, ts, rw_shift_mu[l], sb_q_norm_g[l], sb_k_norm_g[l], rw_w0[l], rw_w2[l],
            rw_a0[l], rw_a2[l], rw_g2[l], rw_k_k[l], rw_k_a[l])
        v = proj[:, 2 * D_GRP:3 * D_GRP]

        o_sb_p = _sb_prompt(q, k, proj, sb_logit_bias[l], bp, tp)
        o_sb_s = _sb_sample(q, k, proj, sb_logit_bias[l], cache_k_t, cache_v_t, page_table, l, n_p, ts)
        o_sb_s = o_sb_s.transpose(0, 2, 1, 3).reshape(nb * ts, D_GRP)

        rw_in = (r, lw, k2, v_rw, a, b, gate)
        o_rw_p, wkv_p = _rwkv_scan(rw_in, rw_r_k[l], rw_ln_g[l], rw_ln_b[l], None, bp, tp, 0, c=64, t_blk=256)
        o_rw_s, wkv_s = _rwkv_scan(rw_in, rw_r_k[l], rw_ln_g[l], rw_ln_b[l], state_wkv[l], nb, ts, n_p, c=ts, t_blk=ts)

        x = _out_proj(x, o_sb_p, o_sb_s, o_rw_p, o_rw_s, sb_out_scale[l], w_out[l].astype(bf16))
        i = l // 2
        if l % 2 == 0:
            x = _ffn(x, norm_ffn_g[l], ffn_w_gate[i].astype(bf16), ffn_w_up[i].astype(bf16), ffn_w_down[i].astype(bf16))
        else:
            x = _moe(x, norm_ffn_g[l], moe_router[i], moe_w_gate[i].astype(bf16), moe_w_up[i].astype(bf16),
                     moe_w_down[i].astype(bf16))

        kp.append(k[:n_p].reshape(bp, tp, N_HEADS, HEAD_DIM))
        vp.append(v[:n_p].reshape(bp, tp, N_HEADS, HEAD_DIM))
        ksm.append(k[n_p:].reshape(nb, ts, N_HEADS, HEAD_DIM))
        vsm.append(v[n_p:].reshape(nb, ts, N_HEADS, HEAD_DIM))
        wp.append(wkv_p)
        wsm.append(wkv_s)
        sp.append(proj[tp - 1:n_p:tp, rw0:])
        ssm.append(proj[n_p + ts - 1::ts, rw0:])

    return (x[:n_p].reshape(bp, tp, d), x[n_p:].reshape(nb, ts, d), jnp.stack(kp), jnp.stack(vp), jnp.stack(ksm),
            jnp.stack(vsm), jnp.stack(wp), jnp.stack(wsm), jnp.stack(sp), jnp.stack(ssm))
```
